```python
import jax, jax.numpy as jnp
from jax import lax
import numpy as np

D_MODEL = 1024
BATCH = 16
SEQ = 2048
DEPTH = 2

D_MIX = D_MODEL
D_CF = D_MIX // 4
D_SC = D_MIX // 4
D_ATT = D_MIX - D_CF - D_SC
HEAD_DIM = 64
N_ATT_HEADS = D_ATT // HEAD_DIM
CF_WIDTH = 31
SC_WIDTH = 3
Q_BLOCK = 128
EPS = 1e-6
SPLITS = (D_CF, D_CF, D_CF, D_SC, D_SC, D_SC, D_SC, D_ATT, D_ATT, D_ATT, D_ATT, N_ATT_HEADS)
N_IN = sum(SPLITS)
SPLIT_IDX = tuple(int(i) for i in np.cumsum(SPLITS)[:-1])

kernel_name = 'hymba_style_conformer_shortconv_fox_hybrid'


def rmsnorm(x, g):
    xf = x.astype(jnp.float32)
    y = xf * lax.rsqrt(jnp.mean(xf * xf, axis=-1, keepdims=True) + EPS)
    return (y * g.astype(jnp.float32)).astype(x.dtype)


def layernorm(x, g, b):
    xf = x.astype(jnp.float32)
    mu = jnp.mean(xf, axis=-1, keepdims=True)
    xc = xf - mu
    y = xc * lax.rsqrt(jnp.mean(xc * xc, axis=-1, keepdims=True) + EPS)
    return (y * g.astype(jnp.float32) + b.astype(jnp.float32)).astype(x.dtype)


def causal_depthwise_conv(x, w):
    width, c = w.shape
    return lax.conv_general_dilated(
        x, w[:, None, :].astype(x.dtype), window_strides=(1,),
        padding=[(width - 1, 0)], dimension_numbers=('NWC', 'WIO', 'NWC'),
        feature_group_count=c)


def forgetting_attention(q, k, v, log_f):
    seq = q.shape[1]
    scale = 1.0 / float(np.sqrt(HEAD_DIM))
    c = jnp.transpose(lax.cumsum(log_f, axis=1), (0, 2, 1))
    outs = []
    for start in range(0, seq, Q_BLOCK):
        end = start + Q_BLOCK
        qb = q[:, start:end].astype(jnp.float32)
        kb = k[:, :end].astype(jnp.float32)
        vb = v[:, :end]
        s = jnp.einsum('bqhd,bkhd->bhqk', qb, kb) * scale
        decay = c[:, :, start:end, None] - c[:, :, None, :end]
        qpos = jnp.arange(start, end)[:, None]
        kpos = jnp.arange(end)[None, :]
        s = jnp.where(kpos <= qpos, s + decay, -jnp.inf)
        p = jax.nn.softmax(s, axis=-1)
        outs.append(jnp.einsum('bhqk,bkhd->bqhd', p.astype(v.dtype), vb))
    return jnp.concatenate(outs, axis=1)


def hybrid_layer(x, norm_g, w_in, b_f, cf_dw, cf_dw_b, cf_ln_g, cf_ln_b, cf_pw,
                 sc_dw, q_norm_g, k_norm_g, w_out):
    bsz, seq, _ = x.shape
    h = rmsnorm(x, norm_g)
    proj = h @ w_in
    (cf_a, cf_g, cf_z, sc_b, sc_c, sc_x, sc_z,
     q, k, v, att_z, f_logit) = jnp.split(proj, SPLIT_IDX, axis=-1)

    u = cf_a * jax.nn.sigmoid(cf_g)
    u = causal_depthwise_conv(u, cf_dw) + cf_dw_b
    u = jax.nn.silu(layernorm(u, cf_ln_g, cf_ln_b))
    y_cf = (u @ cf_pw) * jax.nn.silu(cf_z)

    y_sc = sc_b * causal_depthwise_conv(sc_c * sc_x, sc_dw) * jax.nn.silu(sc_z)

    q = rmsnorm(q.reshape(bsz, seq, N_ATT_HEADS, HEAD_DIM), q_norm_g)
    k = rmsnorm(k.reshape(bsz, seq, N_ATT_HEADS, HEAD_DIM), k_norm_g)
    v = v.reshape(bsz, seq, N_ATT_HEADS, HEAD_DIM)
    log_f = jax.nn.log_sigmoid((f_logit + b_f).astype(jnp.float32))
    o = forgetting_attention(q, k, v, log_f).reshape(bsz, seq, D_ATT)
    y_att = o * jax.nn.silu(att_z)

    mixed = jnp.concatenate([y_cf, y_sc, y_att], axis=-1)
    return x + mixed @ w_out


def setup_inputs(seed: int = 0) -> dict:
    key = jax.random.key(seed)
    ks = jax.random.split(key, 14)
    f32 = jnp.float32
    x = jax.random.normal(ks[0], (BATCH, SEQ, D_MODEL), f32)
    norm_g = 1.0 + 0.05 * jax.random.normal(ks[1], (DEPTH, D_MODEL), f32)
    w_in = jax.random.normal(ks[2], (DEPTH, D_MODEL, N_IN), f32) * D_MODEL ** -0.5
    b_f = 2.0 + 0.5 * jax.random.normal(ks[3], (DEPTH, N_ATT_HEADS), f32)
    cf_dw = jax.random.normal(ks[4], (DEPTH, CF_WIDTH, D_CF), f32) * CF_WIDTH ** -0.5
    cf_dw_b = 0.02 * jax.random.normal(ks[5], (DEPTH, D_CF), f32)
    cf_ln_g = 1.0 + 0.05 * jax.random.normal(ks[6], (DEPTH, D_CF), f32)
    cf_ln_b = 0.02 * jax.random.normal(ks[7], (DEPTH, D_CF), f32)
    cf_pw = jax.random.normal(ks[8], (DEPTH, D_CF, D_CF), f32) * D_CF ** -0.5
    sc_dw = jax.random.normal(ks[9], (DEPTH, SC_WIDTH, D_SC), f32) * SC_WIDTH ** -0.5
    q_norm_g = 1.0 + 0.05 * jax.random.normal(ks[10], (DEPTH, N_ATT_HEADS, HEAD_DIM), f32)
    k_norm_g = 1.0 + 0.05 * jax.random.normal(ks[11], (DEPTH, N_ATT_HEADS, HEAD_DIM), f32)
    w_out = jax.random.normal(ks[12], (DEPTH, D_MIX, D_MODEL), f32) * D_MIX ** -0.5
    return {'x': x, 'norm_g': norm_g, 'w_in': w_in, 'b_f': b_f, 'cf_dw': cf_dw,
            'cf_dw_b': cf_dw_b, 'cf_ln_g': cf_ln_g, 'cf_ln_b': cf_ln_b, 'cf_pw': cf_pw,
            'sc_dw': sc_dw, 'q_norm_g': q_norm_g, 'k_norm_g': k_norm_g, 'w_out': w_out}


def reference(x, norm_g, w_in, b_f, cf_dw, cf_dw_b, cf_ln_g, cf_ln_b, cf_pw,
              sc_dw, q_norm_g, k_norm_g, w_out):
    for l in range(DEPTH):
        x = hybrid_layer(x, norm_g[l], w_in[l], b_f[l], cf_dw[l], cf_dw_b[l],
                         cf_ln_g[l], cf_ln_b[l], cf_pw[l], sc_dw[l],
                         q_norm_g[l], k_norm_g[l], w_out[l])
    return x
```

```python
import functools

import jax
import jax.numpy as jnp
from jax import lax
from jax.experimental import pallas as pl
from jax.experimental.pallas import tpu as pltpu

D_MODEL = 1024
D_CF = 256
D_SC = 256
D_ATT = 512
HEAD_DIM = 64
N_HEADS = D_ATT // HEAD_DIM
CF_WIDTH = 31
SC_WIDTH = 3
EPS = 1e-6
N_NAT = 3 * D_CF + 4 * D_SC
N_T = 4 * D_ATT + 16

K_AUG = 128
V_AUG = 80
CF_HALO = 32
SC_HALO = 8
NEG = -1e30

TS = 512
TQ = 512
TK = 256
VMEM_LIMIT = 56 * 1024 * 1024


def _sigmoid(x):
    return 1.0 / (1.0 + jnp.exp(-x))


def _silu(x):
    return x * _sigmoid(x)


def _split3(c):
    hi = c.astype(jnp.bfloat16).astype(jnp.float32)
    r = c - hi
    mid = r.astype(jnp.bfloat16).astype(jnp.float32)
    lo = (r - mid).astype(jnp.bfloat16).astype(jnp.float32)
    return hi, mid, lo


def _inproj_kernel(x_ref, ng_ref, wn_ref, wt_ref, bf_ref, cfdw_ref, cfb_ref, lng_ref, lnb_ref,
                   cfpw_ref, scdw_ref, gq_ref, gk_ref,
                   ycs_ref, qa_ref, ka_ref, va_ref, g_ref,
                   ubuf, wbuf, ccarry, *, ts, tk):
    j = pl.program_id(1)

    @pl.when(j == 0)
    def _():
        ubuf[0:CF_HALO, :] = jnp.zeros((CF_HALO, D_CF), jnp.float32)
        wbuf[0:SC_HALO, :] = jnp.zeros((SC_HALO, D_SC), jnp.float32)
        ccarry[...] = jnp.zeros_like(ccarry)

    xf = x_ref[0]
    ms = jnp.mean(xf * xf, axis=-1, keepdims=True)
    hb = (xf * lax.rsqrt(ms + EPS) * ng_ref[...]).astype(jnp.bfloat16)

    p1 = jnp.dot(hb, wn_ref[:, 0:3 * D_CF], preferred_element_type=jnp.float32)
    cf_a = p1[:, 0:D_CF]
    cf_g = p1[:, D_CF:2 * D_CF]
    cf_z = p1[:, 2 * D_CF:3 * D_CF]
    ubuf[CF_HALO:CF_HALO + ts, :] = cf_a * _sigmoid(cf_g)
    conv = jnp.broadcast_to(cfb_ref[...], (ts, D_CF))
    for k in range(CF_WIDTH):
        off = CF_HALO - (CF_WIDTH - 1) + k
        conv = conv + cfdw_ref[k:k + 1, :] * ubuf[off:off + ts, :]
    ubuf[0:CF_HALO, :] = ubuf[ts:ts + CF_HALO, :]
    mu = jnp.mean(conv, axis=-1, keepdims=True)
    xc = conv - mu
    var = jnp.mean(xc * xc, axis=-1, keepdims=True)
    ln = xc * lax.rsqrt(var + EPS) * lng_ref[...] + lnb_ref[...]
    y_cf = jnp.dot(_silu(ln).astype(jnp.bfloat16), cfpw_ref[...],
                   preferred_element_type=jnp.float32) * _silu(cf_z)
    ycs_ref[0, :, 0:D_CF] = y_cf.astype(ycs_ref.dtype)

    p2 = jnp.dot(hb, wn_ref[:, 3 * D_CF:N_NAT], preferred_element_type=jnp.float32)
    sc_b = p2[:, 0:D_SC]
    sc_c = p2[:, D_SC:2 * D_SC]
    sc_x = p2[:, 2 * D_SC:3 * D_SC]
    sc_z = p2[:, 3 * D_SC:4 * D_SC]
    wbuf[SC_HALO:SC_HALO + ts, :] = sc_c * sc_x
    sconv = jnp.zeros((ts, D_SC), jnp.float32)
    for k in range(SC_WIDTH):
        off = SC_HALO - (SC_WIDTH - 1) + k
        sconv = sconv + scdw_ref[k:k + 1, :] * wbuf[off:off + ts, :]
    wbuf[0:SC_HALO, :] = wbuf[ts:ts + SC_HALO, :]
    ycs_ref[0, :, D_CF:D_CF + D_SC] = (sc_b * sconv * _silu(sc_z)).astype(ycs_ref.dtype)

    nt = (((1,), (1,)), ((), ()))

    ft = lax.dot_general(wt_ref[4 * D_ATT:N_T, :], hb, nt,
                         preferred_element_type=jnp.float32)[0:N_HEADS]
    z = ft + bf_ref[...]
    c = jnp.minimum(z, 0.0) - jnp.log1p(jnp.exp(-jnp.abs(z)))
    lane = lax.broadcasted_iota(jnp.int32, (N_HEADS, ts), 1)
    d = 1
    while d < ts:
        c = c + jnp.where(lane >= d, pltpu.roll(c, d, axis=1), 0.0)
        d *= 2
    c = c + ccarry[:, 0:1]
    ccarry[...] = jnp.broadcast_to(c[:, ts - 1:ts], ccarry.shape)

    row16 = lax.broadcasted_iota(jnp.int32, (16, ts), 0)
    zpad = jnp.zeros((K_AUG - HEAD_DIM - 16, ts), jnp.float32)

    def head_norm(t, g_ref):
        t3 = t.reshape(N_HEADS, HEAD_DIM, ts)
        ssq = jnp.mean(t3 * t3, axis=1, keepdims=True)
        return (t3 * lax.rsqrt(ssq + EPS)).reshape(D_ATT, ts) * g_ref[...]

    qn = head_norm(lax.dot_general(wt_ref[0:D_ATT, :], hb, nt,
                                   preferred_element_type=jnp.float32), gq_ref)
    kn = head_norm(lax.dot_general(wt_ref[D_ATT:2 * D_ATT, :], hb, nt,
                                   preferred_element_type=jnp.float32), gk_ref)
    vt = lax.dot_general(wt_ref[2 * D_ATT:3 * D_ATT, :], hb, nt,
                         preferred_element_type=jnp.float32)
    zt = lax.dot_general(wt_ref[3 * D_ATT:4 * D_ATT, :], hb, nt,
                         preferred_element_type=jnp.float32)
    g_ref[0] = _silu(zt).astype(g_ref.dtype)

    ones_row = jnp.where(
        lax.broadcasted_iota(jnp.int32, (V_AUG - HEAD_DIM, tk), 0) == 0, 1.0, 0.0)
    row8 = lax.broadcasted_iota(jnp.int32, (N_HEADS, ts), 0)
    for h in range(N_HEADS):
        c_h = jnp.sum(jnp.where(row8 == h, c, 0.0), axis=0, keepdims=True)
        hi, mid, lo = _split3(c_h)
        csplit = jnp.where(row16 == 0, hi, jnp.where(row16 == 1, mid, lo))
        q_ext = jnp.where(row16 < 3, csplit, jnp.where(row16 < 6, 1.0, 0.0))
        row_m3 = jnp.where(row16 == 3, hi, jnp.where(row16 == 4, mid, lo))
        k_ext = jnp.where(row16 < 3, 1.0, jnp.where(row16 < 6, -row_m3, 0.0))
        lo_h, hi_h = h * HEAD_DIM, (h + 1) * HEAD_DIM
        qa_ref[0, h, 0:HEAD_DIM, :] = qn[lo_h:hi_h].astype(qa_ref.dtype)
        qa_ref[0, h, HEAD_DIM:HEAD_DIM + 16, :] = q_ext.astype(qa_ref.dtype)
        qa_ref[0, h, HEAD_DIM + 16:K_AUG, :] = zpad.astype(qa_ref.dtype)
        k_aug_t = jnp.concatenate([kn[lo_h:hi_h], k_ext, zpad], axis=0)
        ka_ref[0, h] = k_aug_t.T.astype(ka_ref.dtype)
        for s in range(ts // tk):
            va_ref[0, h, s, 0:HEAD_DIM, :] = vt[lo_h:hi_h, s * tk:(s + 1) * tk].astype(va_ref.dtype)
            va_ref[0, h, s, HEAD_DIM:V_AUG, :] = ones_row.astype(va_ref.dtype)


def _attn_kernel(qa_ref, ka_ref, va_ref, g_ref, y_ref, *, tq, tk):
    i = pl.program_id(2)
    qa = qa_ref[0, 0]
    steps = tq // tk

    def step(j, carry, masked):
        m, acc = carry
        k0 = pl.multiple_of(j * tk, tk)
        s = jnp.dot(ka_ref[0, 0, pl.ds(k0, tk), :], qa, preferred_element_type=jnp.float32)
        if masked:
            kpos = k0 + lax.broadcasted_iota(jnp.int32, (tk, tq), 0)
            qpos = i * tq + lax.broadcasted_iota(jnp.int32, (tk, tq), 1)
            s = jnp.where(kpos <= qpos, s, NEG)
        m_new = jnp.maximum(m, jnp.max(s, axis=0, keepdims=True))
        p = jnp.exp(s - m_new).astype(jnp.bfloat16)
        alpha = jnp.exp(m - m_new)
        acc = alpha * acc + jnp.dot(va_ref[0, 0, j], p, preferred_element_type=jnp.float32)
        return m_new, acc

    carry = (jnp.full((1, tq), NEG, jnp.float32), jnp.zeros((V_AUG, tq), jnp.float32))
    carry = lax.fori_loop(0, i * steps, functools.partial(step, masked=False), carry)
    for jj in range(steps):
        carry = step(i * steps + jj, carry, True)
    _, acc = carry
    o = acc[0:HEAD_DIM] / acc[HEAD_DIM:HEAD_DIM + 1]
    y_ref[0] = (o * g_ref[0].astype(jnp.float32)).astype(y_ref.dtype)


def _outproj_kernel(x_ref, ycs_ref, yt_ref, woa_ref, wob_ref, o_ref):
    tn = (((0,), (0,)), ((), ()))
    acc = jnp.dot(ycs_ref[0], woa_ref[...], preferred_element_type=jnp.float32)
    acc = acc + lax.dot_general(yt_ref[0], wob_ref[...], tn, preferred_element_type=jnp.float32)
    o_ref[0] = x_ref[0] + acc


def _const_spec(shape):
    return pl.BlockSpec(shape, lambda *_: (0,) * len(shape))


def _inproj_call(x, ng, wn, wt, bf, cfdw, cfb, lng, lnb, cfpw, scdw, gq, gk, *, ts, tk):
    bsz, seq, _ = x.shape
    bf16 = jnp.bfloat16
    out_shape = (
        jax.ShapeDtypeStruct((bsz, seq, D_CF + D_SC), bf16),
        jax.ShapeDtypeStruct((bsz, N_HEADS, K_AUG, seq), bf16),
        jax.ShapeDtypeStruct((bsz, N_HEADS, seq, K_AUG), bf16),
        jax.ShapeDtypeStruct((bsz, N_HEADS, seq // tk, V_AUG, tk), bf16),
        jax.ShapeDtypeStruct((bsz, D_ATT, seq), bf16),
    )
    in_specs = [
        pl.BlockSpec((1, ts, D_MODEL), lambda b, j: (b, j, 0)),
        _const_spec(ng.shape), _const_spec(wn.shape), _const_spec(wt.shape), _const_spec(bf.shape),
        _const_spec(cfdw.shape), _const_spec(cfb.shape), _const_spec(lng.shape),
        _const_spec(lnb.shape), _const_spec(cfpw.shape), _const_spec(scdw.shape),
        _const_spec(gq.shape), _const_spec(gk.shape),
    ]
    out_specs = (
        pl.BlockSpec((1, ts, D_CF + D_SC), lambda b, j: (b, j, 0)),
        pl.BlockSpec((1, N_HEADS, K_AUG, ts), lambda b, j: (b, 0, 0, j)),
        pl.BlockSpec((1, N_HEADS, ts, K_AUG), lambda b, j: (b, 0, j, 0)),
        pl.BlockSpec((1, N_HEADS, ts // tk, V_AUG, tk), lambda b, j: (b, 0, j, 0, 0)),
        pl.BlockSpec((1, D_ATT, ts), lambda b, j: (b, 0, j)),
    )
    return pl.pallas_call(
        functools.partial(_inproj_kernel, ts=ts, tk=tk),
        grid=(bsz, seq // ts),
        in_specs=in_specs,
        out_specs=out_specs,
        out_shape=out_shape,
        scratch_shapes=[
            pltpu.VMEM((ts + CF_HALO, D_CF), jnp.float32),
            pltpu.VMEM((ts + SC_HALO, D_SC), jnp.float32),
            pltpu.VMEM((N_HEADS, 128), jnp.float32),
        ],
        compiler_params=pltpu.CompilerParams(
            dimension_semantics=("arbitrary", "arbitrary"), vmem_limit_bytes=VMEM_LIMIT),
        name="inproj",
    )(x, ng, wn, wt, bf, cfdw, cfb, lng, lnb, cfpw, scdw, gq, gk)


def _attn_call(qa, ka, va, g, *, tq, tk):
    bsz, _, _, seq = qa.shape
    return pl.pallas_call(
        functools.partial(_attn_kernel, tq=tq, tk=tk),
        grid=(bsz, N_HEADS, seq // tq),
        in_specs=[
            pl.BlockSpec((1, 1, K_AUG, tq), lambda b, h, i: (b, h, 0, i)),
            pl.BlockSpec((1, 1, seq, K_AUG), lambda b, h, i: (b, h, 0, 0)),
            pl.BlockSpec((1, 1, seq // tk, V_AUG, tk), lambda b, h, i: (b, h, 0, 0, 0)),
            pl.BlockSpec((1, HEAD_DIM, tq), lambda b, h, i: (b, h, i)),
        ],
        out_specs=pl.BlockSpec((1, HEAD_DIM, tq), lambda b, h, i: (b, h, i)),
        out_shape=jax.ShapeDtypeStruct((bsz, D_ATT, seq), jnp.bfloat16),
        compiler_params=pltpu.CompilerParams(
            dimension_semantics=("arbitrary", "arbitrary", "arbitrary"),
            vmem_limit_bytes=VMEM_LIMIT),
        name="fox_attention",
    )(qa, ka, va, g)


def _outproj_call(x, ycs, yt, woa, wob, *, ts):
    bsz, seq, _ = x.shape
    return pl.pallas_call(
        _outproj_kernel,
        grid=(bsz, seq // ts),
        in_specs=[
            pl.BlockSpec((1, ts, D_MODEL), lambda b, j: (b, j, 0)),
            pl.BlockSpec((1, ts, D_CF + D_SC), lambda b, j: (b, j, 0)),
            pl.BlockSpec((1, D_ATT, ts), lambda b, j: (b, 0, j)),
            _const_spec(woa.shape), _const_spec(wob.shape),
        ],
        out_specs=pl.BlockSpec((1, ts, D_MODEL), lambda b, j: (b, j, 0)),
        out_shape=jax.ShapeDtypeStruct(x.shape, x.dtype),
        compiler_params=pltpu.CompilerParams(
            dimension_semantics=("arbitrary", "arbitrary"), vmem_limit_bytes=VMEM_LIMIT),
        name="outproj",
    )(x, ycs, yt, woa, wob)


def _layer(x, norm_g, w_in, b_f, cf_dw, cf_dw_b, cf_ln_g, cf_ln_b, cf_pw, sc_dw,
           q_norm_g, k_norm_g, w_out):
    seq = x.shape[1]
    ts = min(TS, seq)
    tq = min(TQ, seq)
    tk = min(TK, tq)
    assert seq % ts == 0 and seq % tq == 0 and ts % tk == 0 and tq % tk == 0
    bf16 = jnp.bfloat16
    f32 = jnp.float32
    wn = w_in[:, :N_NAT].astype(bf16)
    wt = jnp.pad(w_in[:, N_NAT:].T, ((0, N_T - (4 * D_ATT + N_HEADS)), (0, 0))).astype(bf16)
    scale = 1.0 / (HEAD_DIM ** 0.5)
    gq = (q_norm_g.astype(f32) * scale).reshape(D_ATT, 1)
    gk = k_norm_g.astype(f32).reshape(D_ATT, 1)
    ycs, qa, ka, va, g = _inproj_call(
        x, norm_g.reshape(1, D_MODEL), wn, wt, b_f.reshape(N_HEADS, 1),
        cf_dw, cf_dw_b.reshape(1, D_CF), cf_ln_g.reshape(1, D_CF), cf_ln_b.reshape(1, D_CF),
        cf_pw.astype(bf16), sc_dw, gq, gk, ts=ts, tk=tk)
    yt = _attn_call(qa, ka, va, g, tq=tq, tk=tk)
    woa = w_out[:D_CF + D_SC].astype(bf16)
    wob = w_out[D_CF + D_SC:].astype(bf16)
    return _outproj_call(x, ycs, yt, woa, wob, ts=ts)


def kernel(x, norm_g, w_in, b_f, cf_dw, cf_dw_b, cf_ln_g, cf_ln_b, cf_pw, sc_dw, q_norm_g, k_norm_g, w_out):
    for l in range(norm_g.shape[0]):
        x = _layer(x, norm_g[l], w_in[l], b_f[l], cf_dw[l], cf_dw_b[l], cf_ln_g[l], cf_ln_b[l],
                   cf_pw[l], sc_dw[l], q_norm_g[l], k_norm_g[l], w_out[l])
    return x
```

```python
import functools

import jax
import jax.numpy as jnp
from jax import lax
from jax.experimental import pallas as pl
from jax.experimental.pallas import tpu as pltpu

D_MODEL = 1024
D_CF = 256
D_SC = 256
D_ATT = 512
HEAD_DIM = 64
N_HEADS = D_ATT // HEAD_DIM
CF_WIDTH = 31
SC_WIDTH = 3
EPS = 1e-6
N_NAT = 3 * D_CF + 4 * D_SC
N_T = 4 * D_ATT + 16

K_AUG = 128
V_AUG = 80
CF_HALO = 32
SC_HALO = 8
NEG = -1e30
LOG2E = 1.4426950408889634

TS = 512
TQ = 512
VMEM_LIMIT = 56 * 1024 * 1024


def _sigmoid(x):
    return 1.0 / (1.0 + jnp.exp(-x))


def _silu(x):
    return x * _sigmoid(x)


def _split3(c):
    hi = c.astype(jnp.bfloat16).astype(jnp.float32)
    r = c - hi
    mid = r.astype(jnp.bfloat16).astype(jnp.float32)
    lo = (r - mid).astype(jnp.bfloat16).astype(jnp.float32)
    return hi, mid, lo


def _inproj_kernel(x_ref, ng_ref, wn_ref, wt_ref, bf_ref, cfdw_ref, cfb_ref, lng_ref, lnb_ref,
                   cfpw_ref, scdw_ref, gq_ref, gk_ref,
                   ycs_ref, qa_ref, ka_ref, va_ref, g_ref,
                   ubuf, wbuf, ccarry, *, ts):
    j = pl.program_id(1)

    @pl.when(j == 0)
    def _():
        ubuf[0:CF_HALO, :] = jnp.zeros((CF_HALO, D_CF), jnp.float32)
        wbuf[0:SC_HALO, :] = jnp.zeros((SC_HALO, D_SC), jnp.float32)
        ccarry[...] = jnp.zeros_like(ccarry)

    xf = x_ref[0]
    ms = jnp.mean(xf * xf, axis=-1, keepdims=True)
    hb = (xf * lax.rsqrt(ms + EPS) * ng_ref[...]).astype(jnp.bfloat16)

    p1 = jnp.dot(hb, wn_ref[:, 0:3 * D_CF], preferred_element_type=jnp.float32)
    cf_a = p1[:, 0:D_CF]
    cf_g = p1[:, D_CF:2 * D_CF]
    cf_z = p1[:, 2 * D_CF:3 * D_CF]
    ubuf[CF_HALO:CF_HALO + ts, :] = cf_a * _sigmoid(cf_g)
    conv = jnp.broadcast_to(cfb_ref[...], (ts, D_CF))
    base = CF_HALO - (CF_WIDTH - 1)
    for r in range(8):
        taps = [k for k in range(CF_WIDTH) if (base + k) % 8 == r]
        if not taps:
            continue
        a_lo = (base + taps[0]) // 8
        a_hi = (base + taps[-1]) // 8
        shifted = ubuf[8 * a_lo + r:8 * a_hi + r + ts, :]
        for k in taps:
            a = (base + k) // 8 - a_lo
            conv = conv + cfdw_ref[k:k + 1, :] * shifted[8 * a:8 * a + ts, :]
    ubuf[0:CF_HALO, :] = ubuf[ts:ts + CF_HALO, :]
    mu = jnp.mean(conv, axis=-1, keepdims=True)
    xc = conv - mu
    var = jnp.mean(xc * xc, axis=-1, keepdims=True)
    ln = xc * lax.rsqrt(var + EPS) * lng_ref[...] + lnb_ref[...]
    y_cf = jnp.dot(_silu(ln).astype(jnp.bfloat16), cfpw_ref[...],
                   preferred_element_type=jnp.float32) * _silu(cf_z)
    ycs_ref[0, :, 0:D_CF] = y_cf.astype(ycs_ref.dtype)

    p2 = jnp.dot(hb, wn_ref[:, 3 * D_CF:N_NAT], preferred_element_type=jnp.float32)
    sc_b = p2[:, 0:D_SC]
    sc_c = p2[:, D_SC:2 * D_SC]
    sc_x = p2[:, 2 * D_SC:3 * D_SC]
    sc_z = p2[:, 3 * D_SC:4 * D_SC]
    wbuf[SC_HALO:SC_HALO + ts, :] = sc_c * sc_x
    sconv = jnp.zeros((ts, D_SC), jnp.float32)
    for k in range(SC_WIDTH):
        off = SC_HALO - (SC_WIDTH - 1) + k
        sconv = sconv + scdw_ref[k:k + 1, :] * wbuf[off:off + ts, :]
    wbuf[0:SC_HALO, :] = wbuf[ts:ts + SC_HALO, :]
    ycs_ref[0, :, D_CF:D_CF + D_SC] = (sc_b * sconv * _silu(sc_z)).astype(ycs_ref.dtype)

    nt = (((1,), (1,)), ((), ()))

    ft = lax.dot_general(wt_ref[4 * D_ATT:N_T, :], hb, nt,
                         preferred_element_type=jnp.float32)[0:N_HEADS]
    z = ft + bf_ref[...]
    c = jnp.minimum(z, 0.0) - jnp.log1p(jnp.exp(-jnp.abs(z)))
    lane = lax.broadcasted_iota(jnp.int32, (N_HEADS, ts), 1)
    d = 1
    while d < ts:
        c = c + jnp.where(lane >= d, pltpu.roll(c, d, axis=1), 0.0)
        d *= 2
    c = c + ccarry[:, 0:1]
    ccarry[...] = jnp.broadcast_to(c[:, ts - 1:ts], ccarry.shape)
    c = c * LOG2E

    row16 = lax.broadcasted_iota(jnp.int32, (16, ts), 0)
    zpad = jnp.zeros((K_AUG - HEAD_DIM - 16, ts), jnp.float32)

    def head_norm(t, g_ref):
        t3 = t.reshape(N_HEADS, HEAD_DIM, ts)
        ssq = jnp.mean(t3 * t3, axis=1, keepdims=True)
        return (t3 * lax.rsqrt(ssq + EPS)).reshape(D_ATT, ts) * g_ref[...]

    qn = head_norm(lax.dot_general(wt_ref[0:D_ATT, :], hb, nt,
                                   preferred_element_type=jnp.float32), gq_ref)
    kn = head_norm(lax.dot_general(wt_ref[D_ATT:2 * D_ATT, :], hb, nt,
                                   preferred_element_type=jnp.float32), gk_ref)
    vt = lax.dot_general(wt_ref[2 * D_ATT:3 * D_ATT, :], hb, nt,
                         preferred_element_type=jnp.float32)
    zt = lax.dot_general(wt_ref[3 * D_ATT:4 * D_ATT, :], hb, nt,
                         preferred_element_type=jnp.float32)
    g_ref[0] = _silu(zt).astype(g_ref.dtype).reshape(N_HEADS, HEAD_DIM, ts)

    ones_row = jnp.where(
        lax.broadcasted_iota(jnp.int32, (V_AUG - HEAD_DIM, ts), 0) == 0, 1.0, 0.0)
    row8 = lax.broadcasted_iota(jnp.int32, (N_HEADS, ts), 0)
    for h in range(N_HEADS):
        c_h = jnp.sum(jnp.where(row8 == h, c, 0.0), axis=0, keepdims=True)
        hi, mid, lo = _split3(c_h)
        csplit = jnp.where(row16 == 0, hi, jnp.where(row16 == 1, mid, lo))
        q_ext = jnp.where(row16 < 3, csplit, jnp.where(row16 < 6, 1.0, 0.0))
        row_m3 = jnp.where(row16 == 3, hi, jnp.where(row16 == 4, mid, lo))
        k_ext = jnp.where(row16 < 3, 1.0, jnp.where(row16 < 6, -row_m3, 0.0))
        lo_h, hi_h = h * HEAD_DIM, (h + 1) * HEAD_DIM
        qa_ref[0, h, 0:HEAD_DIM, :] = qn[lo_h:hi_h].astype(qa_ref.dtype)
        qa_ref[0, h, HEAD_DIM:HEAD_DIM + 16, :] = q_ext.astype(qa_ref.dtype)
        qa_ref[0, h, HEAD_DIM + 16:K_AUG, :] = zpad.astype(qa_ref.dtype)
        k_aug_t = jnp.concatenate([kn[lo_h:hi_h], k_ext, zpad], axis=0)
        ka_ref[0, h] = k_aug_t.T.astype(ka_ref.dtype)
        va_ref[0, h, 0:HEAD_DIM, :] = vt[lo_h:hi_h].astype(va_ref.dtype)
        va_ref[0, h, HEAD_DIM:V_AUG, :] = ones_row.astype(va_ref.dtype)


def _attn_kernel(qa_ref, ka_ref, va_ref, g_ref, y_ref, s_scr, m_scr, *, seq, tq):
    nq = seq // tq
    kpos = lax.broadcasted_iota(jnp.int32, (tq, tq), 0)
    qpos = lax.broadcasted_iota(jnp.int32, (tq, tq), 1)
    causal = kpos <= qpos

    def scores(h, i, slot):
        q0 = i * tq
        qa = qa_ref[0, h, :, q0:q0 + tq]
        s_d = jnp.dot(ka_ref[0, h, q0:q0 + tq, :], qa, preferred_element_type=jnp.float32)
        s_d = jnp.where(causal, s_d, NEG)
        s_scr[slot, q0:q0 + tq, :] = s_d
        m = jnp.max(s_d, axis=0, keepdims=True)
        if i > 0:
            s_o = jnp.dot(ka_ref[0, h, 0:q0, :], qa, preferred_element_type=jnp.float32)
            s_scr[slot, 0:q0, :] = s_o
            m = jnp.maximum(m, jnp.max(s_o, axis=0, keepdims=True))
        m_scr[slot] = jnp.broadcast_to(m, (8, tq))

    def finish(h, i, slot):
        q0 = i * tq
        m = m_scr[slot, 0:1, :]
        p = jnp.exp2(s_scr[slot, 0:q0 + tq, :] - m).astype(jnp.bfloat16)
        acc = jnp.dot(va_ref[0, h, :, 0:q0 + tq], p, preferred_element_type=jnp.float32)
        o = acc[0:HEAD_DIM] / acc[HEAD_DIM:HEAD_DIM + 1]
        gate = g_ref[0, h, :, q0:q0 + tq].astype(jnp.float32)
        y_ref[0, h, :, q0:q0 + tq] = (o * gate).astype(y_ref.dtype)

    order = list(range(nq - 1, -1, -1))
    scores(0, order[0], 0)

    def head(h, carry):
        h_next = jnp.minimum(h + 1, N_HEADS - 1)
        for n, i in enumerate(order):
            if n + 1 < nq:
                scores(h, order[n + 1], (n + 1) % 2)
            else:
                scores(h_next, order[0], (n + 1) % 2)
            finish(h, i, n % 2)
        return carry

    lax.fori_loop(0, N_HEADS, head, 0)


def _outproj_kernel(x_ref, ycs_ref, yt_ref, woa_ref, wob_ref, o_ref):
    tn = (((0,), (0,)), ((), ()))
    acc = jnp.dot(ycs_ref[0], woa_ref[...], preferred_element_type=jnp.float32)
    acc = acc + lax.dot_general(yt_ref[0], wob_ref[...], tn, preferred_element_type=jnp.float32)
    o_ref[0] = x_ref[0] + acc


def _const_spec(shape):
    return pl.BlockSpec(shape, lambda *_: (0,) * len(shape))


def _inproj_call(x, ng, wn, wt, bf, cfdw, cfb, lng, lnb, cfpw, scdw, gq, gk, *, ts):
    bsz, seq, _ = x.shape
    bf16 = jnp.bfloat16
    out_shape = (
        jax.ShapeDtypeStruct((bsz, seq, D_CF + D_SC), bf16),
        jax.ShapeDtypeStruct((bsz, N_HEADS, K_AUG, seq), bf16),
        jax.ShapeDtypeStruct((bsz, N_HEADS, seq, K_AUG), bf16),
        jax.ShapeDtypeStruct((bsz, N_HEADS, V_AUG, seq), bf16),
        jax.ShapeDtypeStruct((bsz, N_HEADS, HEAD_DIM, seq), bf16),
    )
    in_specs = [
        pl.BlockSpec((1, ts, D_MODEL), lambda b, j: (b, j, 0)),
        _const_spec(ng.shape), _const_spec(wn.shape), _const_spec(wt.shape), _const_spec(bf.shape),
        _const_spec(cfdw.shape), _const_spec(cfb.shape), _const_spec(lng.shape),
        _const_spec(lnb.shape), _const_spec(cfpw.shape), _const_spec(scdw.shape),
        _const_spec(gq.shape), _const_spec(gk.shape),
    ]
    out_specs = (
        pl.BlockSpec((1, ts, D_CF + D_SC), lambda b, j: (b, j, 0)),
        pl.BlockSpec((1, N_HEADS, K_AUG, ts), lambda b, j: (b, 0, 0, j)),
        pl.BlockSpec((1, N_HEADS, ts, K_AUG), lambda b, j: (b, 0, j, 0)),
        pl.BlockSpec((1, N_HEADS, V_AUG, ts), lambda b, j: (b, 0, 0, j)),
        pl.BlockSpec((1, N_HEADS, HEAD_DIM, ts), lambda b, j: (b, 0, 0, j)),
    )
    return pl.pallas_call(
        functools.partial(_inproj_kernel, ts=ts),
        grid=(bsz, seq // ts),
        in_specs=in_specs,
        out_specs=out_specs,
        out_shape=out_shape,
        scratch_shapes=[
            pltpu.VMEM((ts + CF_HALO, D_CF), jnp.float32),
            pltpu.VMEM((ts + SC_HALO, D_SC), jnp.float32),
            pltpu.VMEM((N_HEADS, 128), jnp.float32),
        ],
        compiler_params=pltpu.CompilerParams(
            dimension_semantics=("arbitrary", "arbitrary"), vmem_limit_bytes=VMEM_LIMIT),
        name="inproj",
    )(x, ng, wn, wt, bf, cfdw, cfb, lng, lnb, cfpw, scdw, gq, gk)


def _attn_call(qa, ka, va, g, *, tq):
    bsz, _, _, seq = qa.shape

    def whole(arr):
        return pl.BlockSpec((1,) + arr.shape[1:], lambda b: (b, 0, 0, 0))

    return pl.pallas_call(
        functools.partial(_attn_kernel, seq=seq, tq=tq),
        grid=(bsz,),
        in_specs=[whole(qa), whole(ka), whole(va), whole(g)],
        out_specs=pl.BlockSpec((1, N_HEADS, HEAD_DIM, seq), lambda b: (b, 0, 0, 0)),
        out_shape=jax.ShapeDtypeStruct((bsz, N_HEADS, HEAD_DIM, seq), jnp.bfloat16),
        scratch_shapes=[
            pltpu.VMEM((2, seq, tq), jnp.float32),
            pltpu.VMEM((2, 8, tq), jnp.float32),
        ],
        compiler_params=pltpu.CompilerParams(
            dimension_semantics=("arbitrary",), vmem_limit_bytes=VMEM_LIMIT),
        name="fox_attention",
    )(qa, ka, va, g)


def _outproj_call(x, ycs, yt, woa, wob, *, ts):
    bsz, seq, _ = x.shape
    return pl.pallas_call(
        _outproj_kernel,
        grid=(bsz, seq // ts),
        in_specs=[
            pl.BlockSpec((1, ts, D_MODEL), lambda b, j: (b, j, 0)),
            pl.BlockSpec((1, ts, D_CF + D_SC), lambda b, j: (b, j, 0)),
            pl.BlockSpec((1, D_ATT, ts), lambda b, j: (b, 0, j)),
            _const_spec(woa.shape), _const_spec(wob.shape),
        ],
        out_specs=pl.BlockSpec((1, ts, D_MODEL), lambda b, j: (b, j, 0)),
        out_shape=jax.ShapeDtypeStruct(x.shape, x.dtype),
        compiler_params=pltpu.CompilerParams(
            dimension_semantics=("arbitrary", "arbitrary"), vmem_limit_bytes=VMEM_LIMIT),
        name="outproj",
    )(x, ycs, yt, woa, wob)


def _layer(x, norm_g, w_in, b_f, cf_dw, cf_dw_b, cf_ln_g, cf_ln_b, cf_pw, sc_dw,
           q_norm_g, k_norm_g, w_out):
    seq = x.shape[1]
    ts = min(TS, seq)
    tq = min(TQ, seq)
    assert seq % ts == 0 and seq % tq == 0 and (seq // tq) % 2 == 0
    bf16 = jnp.bfloat16
    f32 = jnp.float32
    wn = w_in[:, :N_NAT].astype(bf16)
    wt = jnp.pad(w_in[:, N_NAT:].T, ((0, N_T - (4 * D_ATT + N_HEADS)), (0, 0))).astype(bf16)
    scale = LOG2E / (HEAD_DIM ** 0.5)
    gq = (q_norm_g.astype(f32) * scale).reshape(D_ATT, 1)
    gk = k_norm_g.astype(f32).reshape(D_ATT, 1)
    ycs, qa, ka, va, g = _inproj_call(
        x, norm_g.reshape(1, D_MODEL), wn, wt, b_f.reshape(N_HEADS, 1),
        cf_dw, cf_dw_b.reshape(1, D_CF), cf_ln_g.reshape(1, D_CF), cf_ln_b.reshape(1, D_CF),
        cf_pw.astype(bf16), sc_dw, gq, gk, ts=ts)
    yt = _attn_call(qa, ka, va, g, tq=tq).reshape(x.shape[0], D_ATT, seq)
    woa = w_out[:D_CF + D_SC].astype(bf16)
    wob = w_out[D_CF + D_SC:].astype(bf16)
    return _outproj_call(x, ycs, yt, woa, wob, ts=ts)


def kernel(x, norm_g, w_in, b_f, cf_dw, cf_dw_b, cf_ln_g, cf_ln_b, cf_pw, sc_dw, q_norm_g, k_norm_g, w_out):
    for l in range(norm_g.shape[0]):
        x = _layer(x, norm_g[l], w_in[l], b_f[l], cf_dw[l], cf_dw_b[l], cf_ln_g[l], cf_ln_b[l],
                   cf_pw[l], sc_dw[l], q_norm_g[l], k_norm_g[l], w_out[l])
    return x
```

```python
import functools

import jax
import jax.numpy as jnp
from jax import lax
from jax.experimental import pallas as pl
from jax.experimental.pallas import tpu as pltpu

D_MODEL = 1024
D_CF = 256
D_SC = 256
D_ATT = 512
HEAD_DIM = 64
N_HEADS = D_ATT // HEAD_DIM
CF_WIDTH = 31
SC_WIDTH = 3
EPS = 1e-6
N_NAT = 3 * D_CF + 4 * D_SC
N_T = 4 * D_ATT + 16

K_AUG = 128
V_AUG = 80
CF_HALO = 32
SC_HALO = 8
NEG = -1e30
LOG2E = 1.4426950408889634

TS = 512
TS_OUT = 1024
TQ = 512
VMEM_LIMIT = 56 * 1024 * 1024


def _sigmoid(x):
    return 0.5 * jnp.tanh(0.5 * x) + 0.5


def _silu(x):
    hx = 0.5 * x
    return hx * jnp.tanh(hx) + hx


def _split3(c):
    hi = c.astype(jnp.bfloat16).astype(jnp.float32)
    r = c - hi
    mid = r.astype(jnp.bfloat16).astype(jnp.float32)
    lo = (r - mid).astype(jnp.bfloat16).astype(jnp.float32)
    return hi, mid, lo


def _inproj_kernel(x_ref, ng_ref, wn_ref, wt_ref, bf_ref, cfdw_ref, cfb_ref, lng_ref, lnb_ref,
                   cfpw_ref, scdw_ref, gq_ref, gk_ref,
                   ycs_ref, qa_ref, ka_ref, va_ref, g_ref,
                   ubuf, abuf, wbuf, ccarry, *, ts):
    j = pl.program_id(1)

    @pl.when(j == 0)
    def _():
        ubuf[0:CF_HALO, :] = jnp.zeros((CF_HALO, D_CF), jnp.float32)
        wbuf[0:SC_HALO, :] = jnp.zeros((SC_HALO, D_SC), jnp.float32)
        ccarry[...] = jnp.zeros_like(ccarry)

    xf = x_ref[0]
    ms = jnp.mean(xf * xf, axis=-1, keepdims=True)
    hb = (xf * lax.rsqrt(ms + EPS) * ng_ref[...]).astype(jnp.bfloat16)

    p1 = jnp.dot(hb, wn_ref[:, 0:2 * D_CF], preferred_element_type=jnp.float32)
    ubuf[CF_HALO:CF_HALO + ts, :] = p1[:, 0:D_CF] * _sigmoid(p1[:, D_CF:2 * D_CF])

    p2 = jnp.dot(hb, wn_ref[:, 3 * D_CF:N_NAT], preferred_element_type=jnp.float32)
    sc_b = p2[:, 0:D_SC]
    sc_c = p2[:, D_SC:2 * D_SC]
    sc_x = p2[:, 2 * D_SC:3 * D_SC]
    sc_z = p2[:, 3 * D_SC:4 * D_SC]
    wbuf[SC_HALO:SC_HALO + ts, :] = sc_c * sc_x
    sconv = jnp.zeros((ts, D_SC), jnp.float32)
    for k in range(SC_WIDTH):
        off = SC_HALO - (SC_WIDTH - 1) + k
        sconv = sconv + scdw_ref[k:k + 1, :] * wbuf[off:off + ts, :]
    wbuf[0:SC_HALO, :] = wbuf[ts:ts + SC_HALO, :]
    ycs_ref[0, :, D_CF:D_CF + D_SC] = (sc_b * sconv * _silu(sc_z)).astype(ycs_ref.dtype)

    nt = (((1,), (1,)), ((), ()))

    ft = lax.dot_general(wt_ref[4 * D_ATT:N_T, :], hb, nt,
                         preferred_element_type=jnp.float32)[0:N_HEADS]
    z = ft + bf_ref[...]
    c = jnp.minimum(z, 0.0) - jnp.log1p(jnp.exp(-jnp.abs(z)))
    lane = lax.broadcasted_iota(jnp.int32, (N_HEADS, ts), 1)
    d = 1
    while d < ts:
        c = c + jnp.where(lane >= d, pltpu.roll(c, d, axis=1), 0.0)
        d *= 2
    c = c + ccarry[:, 0:1]
    ccarry[...] = jnp.broadcast_to(c[:, ts - 1:ts], ccarry.shape)
    c = c * LOG2E

    row16 = lax.broadcasted_iota(jnp.int32, (16, ts), 0)
    zpad = jnp.zeros((K_AUG - HEAD_DIM - 16, ts), jnp.float32)

    def head_norm(t, g_ref):
        t3 = t.reshape(N_HEADS, HEAD_DIM, ts)
        ssq = jnp.mean(t3 * t3, axis=1, keepdims=True)
        return (t3 * lax.rsqrt(ssq + EPS)).reshape(D_ATT, ts) * g_ref[...]

    qn = head_norm(lax.dot_general(wt_ref[0:D_ATT, :], hb, nt,
                                   preferred_element_type=jnp.float32), gq_ref)
    kn = head_norm(lax.dot_general(wt_ref[D_ATT:2 * D_ATT, :], hb, nt,
                                   preferred_element_type=jnp.float32), gk_ref)
    vt = lax.dot_general(wt_ref[2 * D_ATT:3 * D_ATT, :], hb, nt,
                         preferred_element_type=jnp.float32)
    zt = lax.dot_general(wt_ref[3 * D_ATT:4 * D_ATT, :], hb, nt,
                         preferred_element_type=jnp.float32)
    g_ref[0] = _silu(zt).astype(g_ref.dtype).reshape(N_HEADS, HEAD_DIM, ts)

    ones_row = jnp.where(
        lax.broadcasted_iota(jnp.int32, (V_AUG - HEAD_DIM, ts), 0) == 0, 1.0, 0.0)
    row8 = lax.broadcasted_iota(jnp.int32, (N_HEADS, ts), 0)
    for h in range(N_HEADS):
        c_h = jnp.sum(jnp.where(row8 == h, c, 0.0), axis=0, keepdims=True)
        hi, mid, lo = _split3(c_h)
        csplit = jnp.where(row16 == 0, hi, jnp.where(row16 == 1, mid, lo))
        q_ext = jnp.where(row16 < 3, csplit, jnp.where(row16 < 6, 1.0, 0.0))
        row_m3 = jnp.where(row16 == 3, hi, jnp.where(row16 == 4, mid, lo))
        k_ext = jnp.where(row16 < 3, 1.0, jnp.where(row16 < 6, -row_m3, 0.0))
        lo_h, hi_h = h * HEAD_DIM, (h + 1) * HEAD_DIM
        qa_ref[0, h, 0:HEAD_DIM, :] = qn[lo_h:hi_h].astype(qa_ref.dtype)
        qa_ref[0, h, HEAD_DIM:HEAD_DIM + 16, :] = q_ext.astype(qa_ref.dtype)
        qa_ref[0, h, HEAD_DIM + 16:K_AUG, :] = zpad.astype(qa_ref.dtype)
        k_aug_t = jnp.concatenate([kn[lo_h:hi_h], k_ext, zpad], axis=0)
        ka_ref[0, h] = k_aug_t.T.astype(ka_ref.dtype)
        va_ref[0, h, 0:HEAD_DIM, :] = vt[lo_h:hi_h].astype(va_ref.dtype)
        va_ref[0, h, HEAD_DIM:V_AUG, :] = ones_row.astype(va_ref.dtype)

    conv = jnp.broadcast_to(cfb_ref[...], (ts, D_CF))
    base = CF_HALO - (CF_WIDTH - 1)
    for r in range(8):
        taps = [k for k in range(CF_WIDTH) if (base + k) % 8 == r]
        if not taps:
            continue
        a_lo = (base + taps[0]) // 8
        a_hi = (base + taps[-1]) // 8
        rows = 8 * (a_hi - a_lo) + ts
        abuf[0:rows, :] = ubuf[8 * a_lo + r:8 * a_lo + r + rows, :]
        for k in taps:
            a = (base + k) // 8 - a_lo
            conv = conv + cfdw_ref[k:k + 1, :] * abuf[8 * a:8 * a + ts, :]
    ubuf[0:CF_HALO, :] = ubuf[ts:ts + CF_HALO, :]
    mu = jnp.mean(conv, axis=-1, keepdims=True)
    xc = conv - mu
    var = jnp.mean(xc * xc, axis=-1, keepdims=True)
    ln = xc * lax.rsqrt(var + EPS) * lng_ref[...] + lnb_ref[...]
    cf_z = jnp.dot(hb, wn_ref[:, 2 * D_CF:3 * D_CF], preferred_element_type=jnp.float32)
    y_cf = jnp.dot(_silu(ln).astype(jnp.bfloat16), cfpw_ref[...],
                   preferred_element_type=jnp.float32) * _silu(cf_z)
    ycs_ref[0, :, 0:D_CF] = y_cf.astype(ycs_ref.dtype)


def _attn_kernel(qa_ref, ka_ref, va_ref, g_ref, y_ref, s_scr, m_scr, *, seq, tq):
    nq = seq // tq
    kpos = lax.broadcasted_iota(jnp.int32, (tq, tq), 0)
    qpos = lax.broadcasted_iota(jnp.int32, (tq, tq), 1)
    causal = kpos <= qpos

    def scores_chunk(h, i, slot, c, m):
        qa = qa_ref[0, h, :, i * tq:(i + 1) * tq]
        s = jnp.dot(ka_ref[0, h, c * tq:(c + 1) * tq, :], qa, preferred_element_type=jnp.float32)
        if c == i:
            s = jnp.where(causal, s, NEG)
        s_scr[slot, c * tq:(c + 1) * tq, :] = s
        mc = jnp.max(s, axis=0, keepdims=True)
        m = mc if m is None else jnp.maximum(m, mc)
        if c == i:
            m_scr[slot] = jnp.broadcast_to(m, (8, tq))
        return m

    def finish_chunk(h, slot, c, acc):
        p = jnp.exp2(s_scr[slot, c * tq:(c + 1) * tq, :] - m_scr[slot, 0:1, :]).astype(jnp.bfloat16)
        d = jnp.dot(va_ref[0, h, :, c * tq:(c + 1) * tq], p, preferred_element_type=jnp.float32)
        return d if acc is None else acc + d

    def pair(h_cur, i_cur, slot_cur, h_nxt, i_nxt, slot_nxt):
        m, acc = None, None
        for c in range(max(i_cur, i_nxt) + 1):
            if c <= i_nxt:
                m = scores_chunk(h_nxt, i_nxt, slot_nxt, c, m)
            if c <= i_cur:
                acc = finish_chunk(h_cur, slot_cur, c, acc)
        q0 = i_cur * tq
        o = acc[0:HEAD_DIM] / acc[HEAD_DIM:HEAD_DIM + 1]
        gate = g_ref[0, h_cur, :, q0:q0 + tq].astype(jnp.float32)
        y_ref[0, h_cur, :, q0:q0 + tq] = (o * gate).astype(y_ref.dtype)

    order = list(range(nq - 1, -1, -1))
    m0 = None
    for c in range(order[0] + 1):
        m0 = scores_chunk(0, order[0], 0, c, m0)

    def head(h, carry):
        h_next = jnp.minimum(h + 1, N_HEADS - 1)
        for n, i in enumerate(order):
            if n + 1 < nq:
                pair(h, i, n % 2, h, order[n + 1], (n + 1) % 2)
            else:
                pair(h, i, n % 2, h_next, order[0], (n + 1) % 2)
        return carry

    lax.fori_loop(0, N_HEADS, head, 0)


def _outproj_kernel(x_ref, ycs_ref, yt_ref, woa_ref, wob_ref, o_ref):
    tn = (((0,), (0,)), ((), ()))
    acc = jnp.dot(ycs_ref[0], woa_ref[...], preferred_element_type=jnp.float32)
    acc = acc + lax.dot_general(yt_ref[0], wob_ref[...], tn, preferred_element_type=jnp.float32)
    o_ref[0] = x_ref[0] + acc


def _const_spec(shape):
    return pl.BlockSpec(shape, lambda *_: (0,) * len(shape))


def _inproj_call(x, ng, wn, wt, bf, cfdw, cfb, lng, lnb, cfpw, scdw, gq, gk, *, ts):
    bsz, seq, _ = x.shape
    bf16 = jnp.bfloat16
    out_shape = (
        jax.ShapeDtypeStruct((bsz, seq, D_CF + D_SC), bf16),
        jax.ShapeDtypeStruct((bsz, N_HEADS, K_AUG, seq), bf16),
        jax.ShapeDtypeStruct((bsz, N_HEADS, seq, K_AUG), bf16),
        jax.ShapeDtypeStruct((bsz, N_HEADS, V_AUG, seq), bf16),
        jax.ShapeDtypeStruct((bsz, N_HEADS, HEAD_DIM, seq), bf16),
    )
    in_specs = [
        pl.BlockSpec((1, ts, D_MODEL), lambda b, j: (b, j, 0)),
        _const_spec(ng.shape), _const_spec(wn.shape), _const_spec(wt.shape), _const_spec(bf.shape),
        _const_spec(cfdw.shape), _const_spec(cfb.shape), _const_spec(lng.shape),
        _const_spec(lnb.shape), _const_spec(cfpw.shape), _const_spec(scdw.shape),
        _const_spec(gq.shape), _const_spec(gk.shape),
    ]
    out_specs = (
        pl.BlockSpec((1, ts, D_CF + D_SC), lambda b, j: (b, j, 0)),
        pl.BlockSpec((1, N_HEADS, K_AUG, ts), lambda b, j: (b, 0, 0, j)),
        pl.BlockSpec((1, N_HEADS, ts, K_AUG), lambda b, j: (b, 0, j, 0)),
        pl.BlockSpec((1, N_HEADS, V_AUG, ts), lambda b, j: (b, 0, 0, j)),
        pl.BlockSpec((1, N_HEADS, HEAD_DIM, ts), lambda b, j: (b, 0, 0, j)),
    )
    return pl.pallas_call(
        functools.partial(_inproj_kernel, ts=ts),
        grid=(bsz, seq // ts),
        in_specs=in_specs,
        out_specs=out_specs,
        out_shape=out_shape,
        scratch_shapes=[
            pltpu.VMEM((ts + CF_HALO, D_CF), jnp.float32),
            pltpu.VMEM((ts + CF_HALO, D_CF), jnp.float32),
            pltpu.VMEM((ts + SC_HALO, D_SC), jnp.float32),
            pltpu.VMEM((N_HEADS, 128), jnp.float32),
        ],
        compiler_params=pltpu.CompilerParams(
            dimension_semantics=("arbitrary", "arbitrary"), vmem_limit_bytes=VMEM_LIMIT),
        name="inproj",
    )(x, ng, wn, wt, bf, cfdw, cfb, lng, lnb, cfpw, scdw, gq, gk)


def _attn_call(qa, ka, va, g, *, tq):
    bsz, _, _, seq = qa.shape

    def whole(arr):
        return pl.BlockSpec((1,) + arr.shape[1:], lambda b: (b, 0, 0, 0))

    return pl.pallas_call(
        functools.partial(_attn_kernel, seq=seq, tq=tq),
        grid=(bsz,),
        in_specs=[whole(qa), whole(ka), whole(va), whole(g)],
        out_specs=pl.BlockSpec((1, N_HEADS, HEAD_DIM, seq), lambda b: (b, 0, 0, 0)),
        out_shape=jax.ShapeDtypeStruct((bsz, N_HEADS, HEAD_DIM, seq), jnp.bfloat16),
        scratch_shapes=[
            pltpu.VMEM((2, seq, tq), jnp.float32),
            pltpu.VMEM((2, 8, tq), jnp.float32),
        ],
        compiler_params=pltpu.CompilerParams(
            dimension_semantics=("arbitrary",), vmem_limit_bytes=VMEM_LIMIT),
        name="fox_attention",
    )(qa, ka, va, g)


def _outproj_call(x, ycs, yt, woa, wob, *, ts):
    bsz, seq, _ = x.shape
    return pl.pallas_call(
        _outproj_kernel,
        grid=(bsz, seq // ts),
        in_specs=[
            pl.BlockSpec((1, ts, D_MODEL), lambda b, j: (b, j, 0)),
            pl.BlockSpec((1, ts, D_CF + D_SC), lambda b, j: (b, j, 0)),
            pl.BlockSpec((1, D_ATT, ts), lambda b, j: (b, 0, j)),
            _const_spec(woa.shape), _const_spec(wob.shape),
        ],
        out_specs=pl.BlockSpec((1, ts, D_MODEL), lambda b, j: (b, j, 0)),
        out_shape=jax.ShapeDtypeStruct(x.shape, x.dtype),
        compiler_params=pltpu.CompilerParams(
            dimension_semantics=("arbitrary", "arbitrary"), vmem_limit_bytes=VMEM_LIMIT),
        name="outproj",
    )(x, ycs, yt, woa, wob)


def _layer(x, norm_g, w_in, b_f, cf_dw, cf_dw_b, cf_ln_g, cf_ln_b, cf_pw, sc_dw,
           q_norm_g, k_norm_g, w_out):
    seq = x.shape[1]
    ts = min(TS, seq)
    tq = min(TQ, seq)
    assert seq % ts == 0 and seq % tq == 0 and (seq // tq) % 2 == 0
    bf16 = jnp.bfloat16
    f32 = jnp.float32
    wn = w_in[:, :N_NAT].astype(bf16)
    wt = jnp.pad(w_in[:, N_NAT:].T, ((0, N_T - (4 * D_ATT + N_HEADS)), (0, 0))).astype(bf16)
    scale = LOG2E / (HEAD_DIM ** 0.5)
    gq = (q_norm_g.astype(f32) * scale).reshape(D_ATT, 1)
    gk = k_norm_g.astype(f32).reshape(D_ATT, 1)
    ycs, qa, ka, va, g = _inproj_call(
        x, norm_g.reshape(1, D_MODEL), wn, wt, b_f.reshape(N_HEADS, 1),
        cf_dw, cf_dw_b.reshape(1, D_CF), cf_ln_g.reshape(1, D_CF), cf_ln_b.reshape(1, D_CF),
        cf_pw.astype(bf16), sc_dw, gq, gk, ts=ts)
    yt = _attn_call(qa, ka, va, g, tq=tq).reshape(x.shape[0], D_ATT, seq)
    woa = w_out[:D_CF + D_SC].astype(bf16)
    wob = w_out[D_CF + D_SC:].astype(bf16)
    return _outproj_call(x, ycs, yt, woa, wob, ts=min(TS_OUT, seq))


def kernel(x, norm_g, w_in, b_f, cf_dw, cf_dw_b, cf_ln_g, cf_ln_b, cf_pw, sc_dw, q_norm_g, k_norm_g, w_out):
    for l in range(norm_g.shape[0]):
        x = _layer(x, norm_g[l], w_in[l], b_f[l], cf_dw[l], cf_dw_b[l], cf_ln_g[l], cf_ln_b[l],
                   cf_pw[l], sc_dw[l], q_norm_g[l], k_norm_g[l], w_out[l])
    return x
```

```python
import functools

import jax
import jax.numpy as jnp
from jax import lax
from jax.experimental import pallas as pl
from jax.experimental.pallas import tpu as pltpu

D_MODEL = 1024
D_CF = 256
D_SC = 256
D_ATT = 512
HEAD_DIM = 64
N_HEADS = D_ATT // HEAD_DIM
CF_WIDTH = 31
SC_WIDTH = 3
EPS = 1e-6
N_NAT = 3 * D_CF + 4 * D_SC
N_T = 4 * D_ATT + 16

K_AUG = 128
V_AUG = 80
CF_HALO = 32
SC_HALO = 8
NEG = -1e30
LOG2E = 1.4426950408889634

TS = 512
TS_OUT = 1024
TQ = 256
HEAD_GROUP = 2
VMEM_LIMIT = 56 * 1024 * 1024


def _sigmoid(x):
    return 0.5 * jnp.tanh(0.5 * x) + 0.5


def _silu(x):
    hx = 0.5 * x
    return hx * jnp.tanh(hx) + hx


def _split3(c):
    hi = c.astype(jnp.bfloat16).astype(jnp.float32)
    r = c - hi
    mid = r.astype(jnp.bfloat16).astype(jnp.float32)
    lo = (r - mid).astype(jnp.bfloat16).astype(jnp.float32)
    return hi, mid, lo


def _inproj_kernel(x_ref, ng_ref, wn_ref, wt_ref, bf_ref, cfdw_ref, cfb_ref, lng_ref, lnb_ref,
                   cfpw_ref, scdw_ref, gq_ref, gk_ref,
                   ycs_ref, qa_ref, ka_ref, va_ref, g_ref,
                   ubuf, abuf, wbuf, ccarry, *, ts):
    j = pl.program_id(1)

    @pl.when(j == 0)
    def _():
        ubuf[0:CF_HALO, :] = jnp.zeros((CF_HALO, D_CF), jnp.float32)
        wbuf[0:SC_HALO, :] = jnp.zeros((SC_HALO, D_SC), jnp.float32)
        ccarry[...] = jnp.zeros_like(ccarry)

    xf = x_ref[0]
    ms = jnp.mean(xf * xf, axis=-1, keepdims=True)
    hb = (xf * lax.rsqrt(ms + EPS) * ng_ref[...]).astype(jnp.bfloat16)

    p1 = jnp.dot(hb, wn_ref[:, 0:2 * D_CF], preferred_element_type=jnp.float32)
    ubuf[CF_HALO:CF_HALO + ts, :] = p1[:, 0:D_CF] * _sigmoid(p1[:, D_CF:2 * D_CF])

    p2 = jnp.dot(hb, wn_ref[:, 3 * D_CF:N_NAT], preferred_element_type=jnp.float32)
    sc_b = p2[:, 0:D_SC]
    sc_c = p2[:, D_SC:2 * D_SC]
    sc_x = p2[:, 2 * D_SC:3 * D_SC]
    sc_z = p2[:, 3 * D_SC:4 * D_SC]
    wbuf[SC_HALO:SC_HALO + ts, :] = sc_c * sc_x
    sconv = jnp.zeros((ts, D_SC), jnp.float32)
    for k in range(SC_WIDTH):
        off = SC_HALO - (SC_WIDTH - 1) + k
        sconv = sconv + scdw_ref[k:k + 1, :] * wbuf[off:off + ts, :]
    wbuf[0:SC_HALO, :] = wbuf[ts:ts + SC_HALO, :]
    ycs_ref[0, :, D_CF:D_CF + D_SC] = (sc_b * sconv * _silu(sc_z)).astype(ycs_ref.dtype)

    nt = (((1,), (1,)), ((), ()))

    ft = lax.dot_general(wt_ref[4 * D_ATT:N_T, :], hb, nt,
                         preferred_element_type=jnp.float32)[0:N_HEADS]
    z = ft + bf_ref[...]
    c = jnp.minimum(z, 0.0) - jnp.log1p(jnp.exp(-jnp.abs(z)))
    lane = lax.broadcasted_iota(jnp.int32, (N_HEADS, ts), 1)
    d = 1
    while d < ts:
        c = c + jnp.where(lane >= d, pltpu.roll(c, d, axis=1), 0.0)
        d *= 2
    c = c + ccarry[:, 0:1]
    ccarry[...] = jnp.broadcast_to(c[:, ts - 1:ts], ccarry.shape)
    c = c * LOG2E

    row16 = lax.broadcasted_iota(jnp.int32, (16, ts), 0)
    zpad = jnp.zeros((K_AUG - HEAD_DIM - 16, ts), jnp.float32)

    def head_norm(t, g_ref):
        t3 = t.reshape(N_HEADS, HEAD_DIM, ts)
        ssq = jnp.mean(t3 * t3, axis=1, keepdims=True)
        return (t3 * lax.rsqrt(ssq + EPS)).reshape(D_ATT, ts) * g_ref[...]

    qn = head_norm(lax.dot_general(wt_ref[0:D_ATT, :], hb, nt,
                                   preferred_element_type=jnp.float32), gq_ref)
    kn = head_norm(lax.dot_general(wt_ref[D_ATT:2 * D_ATT, :], hb, nt,
                                   preferred_element_type=jnp.float32), gk_ref)
    vt = lax.dot_general(wt_ref[2 * D_ATT:3 * D_ATT, :], hb, nt,
                         preferred_element_type=jnp.float32)
    zt = lax.dot_general(wt_ref[3 * D_ATT:4 * D_ATT, :], hb, nt,
                         preferred_element_type=jnp.float32)
    g_ref[0] = _silu(zt).astype(g_ref.dtype).reshape(N_HEADS, HEAD_DIM, ts)

    ones_row = jnp.where(
        lax.broadcasted_iota(jnp.int32, (V_AUG - HEAD_DIM, ts), 0) == 0, 1.0, 0.0)
    row8 = lax.broadcasted_iota(jnp.int32, (N_HEADS, ts), 0)
    for h in range(N_HEADS):
        c_h = jnp.sum(jnp.where(row8 == h, c, 0.0), axis=0, keepdims=True)
        hi, mid, lo = _split3(c_h)
        csplit = jnp.where(row16 == 0, hi, jnp.where(row16 == 1, mid, lo))
        q_ext = jnp.where(row16 < 3, csplit, jnp.where(row16 < 6, 1.0, 0.0))
        row_m3 = jnp.where(row16 == 3, hi, jnp.where(row16 == 4, mid, lo))
        k_ext = jnp.where(row16 < 3, 1.0, jnp.where(row16 < 6, -row_m3, 0.0))
        lo_h, hi_h = h * HEAD_DIM, (h + 1) * HEAD_DIM
        qa_ref[0, h, 0:HEAD_DIM, :] = qn[lo_h:hi_h].astype(qa_ref.dtype)
        qa_ref[0, h, HEAD_DIM:HEAD_DIM + 16, :] = q_ext.astype(qa_ref.dtype)
        qa_ref[0, h, HEAD_DIM + 16:K_AUG, :] = zpad.astype(qa_ref.dtype)
        k_aug_t = jnp.concatenate([kn[lo_h:hi_h], k_ext, zpad], axis=0)
        ka_ref[0, h] = k_aug_t.T.astype(ka_ref.dtype)
        va_ref[0, h, 0:HEAD_DIM, :] = vt[lo_h:hi_h].astype(va_ref.dtype)
        va_ref[0, h, HEAD_DIM:V_AUG, :] = ones_row.astype(va_ref.dtype)

    conv = jnp.broadcast_to(cfb_ref[...], (ts, D_CF))
    base = CF_HALO - (CF_WIDTH - 1)
    for r in range(8):
        taps = [k for k in range(CF_WIDTH) if (base + k) % 8 == r]
        if not taps:
            continue
        a_lo = (base + taps[0]) // 8
        a_hi = (base + taps[-1]) // 8
        rows = 8 * (a_hi - a_lo) + ts
        abuf[0:rows, :] = ubuf[8 * a_lo + r:8 * a_lo + r + rows, :]
        for k in taps:
            a = (base + k) // 8 - a_lo
            conv = conv + cfdw_ref[k:k + 1, :] * abuf[8 * a:8 * a + ts, :]
    ubuf[0:CF_HALO, :] = ubuf[ts:ts + CF_HALO, :]
    mu = jnp.mean(conv, axis=-1, keepdims=True)
    xc = conv - mu
    var = jnp.mean(xc * xc, axis=-1, keepdims=True)
    ln = xc * lax.rsqrt(var + EPS) * lng_ref[...] + lnb_ref[...]
    cf_z = jnp.dot(hb, wn_ref[:, 2 * D_CF:3 * D_CF], preferred_element_type=jnp.float32)
    y_cf = jnp.dot(_silu(ln).astype(jnp.bfloat16), cfpw_ref[...],
                   preferred_element_type=jnp.float32) * _silu(cf_z)
    ycs_ref[0, :, 0:D_CF] = y_cf.astype(ycs_ref.dtype)


def _attn_kernel(qa_ref, ka_ref, va_ref, g_ref, y_ref, s_scr, m_scr, *, seq, tq):
    nq = seq // tq
    kpos = lax.broadcasted_iota(jnp.int32, (tq, tq), 0)
    qpos = lax.broadcasted_iota(jnp.int32, (tq, tq), 1)
    causal = kpos <= qpos

    def scores_chunk(h, i, buf, c, m):
        qa = qa_ref[0, h, :, i * tq:(i + 1) * tq]
        s = jnp.dot(ka_ref[0, h, c * tq:(c + 1) * tq, :], qa, preferred_element_type=jnp.float32)
        if c == i:
            s = jnp.where(causal, s, NEG)
        s_scr[buf, c * tq:(c + 1) * tq, :] = s
        mc = jnp.max(s, axis=0, keepdims=True)
        m = mc if m is None else jnp.maximum(m, mc)
        if c == i:
            m_scr[buf] = jnp.broadcast_to(m, (8, tq))
        return m

    def finish_chunk(h, buf, c, acc):
        p = jnp.exp2(s_scr[buf, c * tq:(c + 1) * tq, :] - m_scr[buf, 0:1, :]).astype(jnp.bfloat16)
        d = jnp.dot(va_ref[0, h, :, c * tq:(c + 1) * tq], p, preferred_element_type=jnp.float32)
        return d if acc is None else acc + d

    def pair(hs_cur, i_cur, slot_cur, hs_nxt, i_nxt, slot_nxt):
        ms = [None] * HEAD_GROUP
        accs = [None] * HEAD_GROUP
        for c in range(max(i_cur, i_nxt) + 1):
            for g in range(HEAD_GROUP):
                if c <= i_nxt:
                    ms[g] = scores_chunk(hs_nxt[g], i_nxt, slot_nxt * HEAD_GROUP + g, c, ms[g])
            for g in range(HEAD_GROUP):
                if c <= i_cur:
                    accs[g] = finish_chunk(hs_cur[g], slot_cur * HEAD_GROUP + g, c, accs[g])
        q0 = i_cur * tq
        for g in range(HEAD_GROUP):
            o = accs[g][0:HEAD_DIM] / accs[g][HEAD_DIM:HEAD_DIM + 1]
            gate = g_ref[0, hs_cur[g], :, q0:q0 + tq].astype(jnp.float32)
            y_ref[0, hs_cur[g], :, q0:q0 + tq] = (o * gate).astype(y_ref.dtype)

    order = list(range(nq - 1, -1, -1))
    for g in range(HEAD_GROUP):
        m0 = None
        for c in range(order[0] + 1):
            m0 = scores_chunk(g, order[0], g, c, m0)

    n_groups = N_HEADS // HEAD_GROUP

    def group(hg, carry):
        hg_next = jnp.minimum(hg + 1, n_groups - 1)
        hs = [hg * HEAD_GROUP + g for g in range(HEAD_GROUP)]
        hs_next = [hg_next * HEAD_GROUP + g for g in range(HEAD_GROUP)]
        for n, i in enumerate(order):
            if n + 1 < nq:
                pair(hs, i, n % 2, hs, order[n + 1], (n + 1) % 2)
            else:
                pair(hs, i, n % 2, hs_next, order[0], (n + 1) % 2)
        return carry

    lax.fori_loop(0, n_groups, group, 0)


def _outproj_kernel(x_ref, ycs_ref, yt_ref, woa_ref, wob_ref, o_ref):
    tn = (((0,), (0,)), ((), ()))
    acc = jnp.dot(ycs_ref[0], woa_ref[...], preferred_element_type=jnp.float32)
    acc = acc + lax.dot_general(yt_ref[0], wob_ref[...], tn, preferred_element_type=jnp.float32)
    o_ref[0] = x_ref[0] + acc


def _const_spec(shape):
    return pl.BlockSpec(shape, lambda *_: (0,) * len(shape))


def _inproj_call(x, ng, wn, wt, bf, cfdw, cfb, lng, lnb, cfpw, scdw, gq, gk, *, ts):
    bsz, seq, _ = x.shape
    bf16 = jnp.bfloat16
    out_shape = (
        jax.ShapeDtypeStruct((bsz, seq, D_CF + D_SC), bf16),
        jax.ShapeDtypeStruct((bsz, N_HEADS, K_AUG, seq), bf16),
        jax.ShapeDtypeStruct((bsz, N_HEADS, seq, K_AUG), bf16),
        jax.ShapeDtypeStruct((bsz, N_HEADS, V_AUG, seq), bf16),
        jax.ShapeDtypeStruct((bsz, N_HEADS, HEAD_DIM, seq), bf16),
    )
    in_specs = [
        pl.BlockSpec((1, ts, D_MODEL), lambda b, j: (b, j, 0)),
        _const_spec(ng.shape), _const_spec(wn.shape), _const_spec(wt.shape), _const_spec(bf.shape),
        _const_spec(cfdw.shape), _const_spec(cfb.shape), _const_spec(lng.shape),
        _const_spec(lnb.shape), _const_spec(cfpw.shape), _const_spec(scdw.shape),
        _const_spec(gq.shape), _const_spec(gk.shape),
    ]
    out_specs = (
        pl.BlockSpec((1, ts, D_CF + D_SC), lambda b, j: (b, j, 0)),
        pl.BlockSpec((1, N_HEADS, K_AUG, ts), lambda b, j: (b, 0, 0, j)),
        pl.BlockSpec((1, N_HEADS, ts, K_AUG), lambda b, j: (b, 0, j, 0)),
        pl.BlockSpec((1, N_HEADS, V_AUG, ts), lambda b, j: (b, 0, 0, j)),
        pl.BlockSpec((1, N_HEADS, HEAD_DIM, ts), lambda b, j: (b, 0, 0, j)),
    )
    return pl.pallas_call(
        functools.partial(_inproj_kernel, ts=ts),
        grid=(bsz, seq // ts),
        in_specs=in_specs,
        out_specs=out_specs,
        out_shape=out_shape,
        scratch_shapes=[
            pltpu.VMEM((ts + CF_HALO, D_CF), jnp.float32),
            pltpu.VMEM((ts + CF_HALO, D_CF), jnp.float32),
            pltpu.VMEM((ts + SC_HALO, D_SC), jnp.float32),
            pltpu.VMEM((N_HEADS, 128), jnp.float32),
        ],
        compiler_params=pltpu.CompilerParams(
            dimension_semantics=("arbitrary", "arbitrary"), vmem_limit_bytes=VMEM_LIMIT),
        name="inproj",
    )(x, ng, wn, wt, bf, cfdw, cfb, lng, lnb, cfpw, scdw, gq, gk)


def _attn_call(qa, ka, va, g, *, tq):
    bsz, _, _, seq = qa.shape

    def whole(arr):
        return pl.BlockSpec((1,) + arr.shape[1:], lambda b: (b, 0, 0, 0))

    return pl.pallas_call(
        functools.partial(_attn_kernel, seq=seq, tq=tq),
        grid=(bsz,),
        in_specs=[whole(qa), whole(ka), whole(va), whole(g)],
        out_specs=pl.BlockSpec((1, N_HEADS, HEAD_DIM, seq), lambda b: (b, 0, 0, 0)),
        out_shape=jax.ShapeDtypeStruct((bsz, N_HEADS, HEAD_DIM, seq), jnp.bfloat16),
        scratch_shapes=[
            pltpu.VMEM((2 * HEAD_GROUP, seq, tq), jnp.float32),
            pltpu.VMEM((2 * HEAD_GROUP, 8, tq), jnp.float32),
        ],
        compiler_params=pltpu.CompilerParams(
            dimension_semantics=("arbitrary",), vmem_limit_bytes=VMEM_LIMIT),
        name="fox_attention",
    )(qa, ka, va, g)


def _outproj_call(x, ycs, yt, woa, wob, *, ts):
    bsz, seq, _ = x.shape
    return pl.pallas_call(
        _outproj_kernel,
        grid=(bsz, seq // ts),
        in_specs=[
            pl.BlockSpec((1, ts, D_MODEL), lambda b, j: (b, j, 0)),
            pl.BlockSpec((1, ts, D_CF + D_SC), lambda b, j: (b, j, 0)),
            pl.BlockSpec((1, D_ATT, ts), lambda b, j: (b, 0, j)),
            _const_spec(woa.shape), _const_spec(wob.shape),
        ],
        out_specs=pl.BlockSpec((1, ts, D_MODEL), lambda b, j: (b, j, 0)),
        out_shape=jax.ShapeDtypeStruct(x.shape, x.dtype),
        compiler_params=pltpu.CompilerParams(
            dimension_semantics=("arbitrary", "arbitrary"), vmem_limit_bytes=VMEM_LIMIT),
        name="outproj",
    )(x, ycs, yt, woa, wob)


def _layer(x, norm_g, w_in, b_f, cf_dw, cf_dw_b, cf_ln_g, cf_ln_b, cf_pw, sc_dw,
           q_norm_g, k_norm_g, w_out):
    seq = x.shape[1]
    ts = min(TS, seq)
    tq = min(TQ, seq)
    assert seq % ts == 0 and seq % tq == 0 and (seq // tq) % 2 == 0
    bf16 = jnp.bfloat16
    f32 = jnp.float32
    wn = w_in[:, :N_NAT].astype(bf16)
    wt = jnp.pad(w_in[:, N_NAT:].T, ((0, N_T - (4 * D_ATT + N_HEADS)), (0, 0))).astype(bf16)
    scale = LOG2E / (HEAD_DIM ** 0.5)
    gq = (q_norm_g.astype(f32) * scale).reshape(D_ATT, 1)
    gk = k_norm_g.astype(f32).reshape(D_ATT, 1)
    ycs, qa, ka, va, g = _inproj_call(
        x, norm_g.reshape(1, D_MODEL), wn, wt, b_f.reshape(N_HEADS, 1),
        cf_dw, cf_dw_b.reshape(1, D_CF), cf_ln_g.reshape(1, D_CF), cf_ln_b.reshape(1, D_CF),
        cf_pw.astype(bf16), sc_dw, gq, gk, ts=ts)
    yt = _attn_call(qa, ka, va, g, tq=tq).reshape(x.shape[0], D_ATT, seq)
    woa = w_out[:D_CF + D_SC].astype(bf16)
    wob = w_out[D_CF + D_SC:].astype(bf16)
    return _outproj_call(x, ycs, yt, woa, wob, ts=min(TS_OUT, seq))


def kernel(x, norm_g, w_in, b_f, cf_dw, cf_dw_b, cf_ln_g, cf_ln_b, cf_pw, sc_dw, q_norm_g, k_norm_g, w_out):
    for l in range(norm_g.shape[0]):
        x = _layer(x, norm_g[l], w_in[l], b_f[l], cf_dw[l], cf_dw_b[l], cf_ln_g[l], cf_ln_b[l],
                   cf_pw[l], sc_dw[l], q_norm_g[l], k_norm_g[l], w_out[l])
    return x
```

```python
import functools

import jax
import jax.numpy as jnp
from jax import lax
from jax.experimental import pallas as pl
from jax.experimental.pallas import tpu as pltpu

D_MODEL = 1024
D_CF = 256
D_SC = 256
D_ATT = 512
HEAD_DIM = 64
N_HEADS = D_ATT // HEAD_DIM
CF_WIDTH = 31
SC_WIDTH = 3
EPS = 1e-6
N_NAT = 3 * D_CF + 4 * D_SC
N_T = 4 * D_ATT + 16

K_AUG = 128
V_AUG = 80
CF_HALO = 32
SC_HALO = 8
NEG = -1e30
LOG2E = 1.4426950408889634

TS = 512
TS_OUT = 2048
TQ = 256
HEAD_GROUP = 2
VMEM_LIMIT = 56 * 1024 * 1024


def _sigmoid(x):
    return 0.5 * jnp.tanh(0.5 * x) + 0.5


def _silu(x):
    hx = 0.5 * x
    return hx * jnp.tanh(hx) + hx


def _split3(c):
    hi = c.astype(jnp.bfloat16).astype(jnp.float32)
    r = c - hi
    mid = r.astype(jnp.bfloat16).astype(jnp.float32)
    lo = (r - mid).astype(jnp.bfloat16).astype(jnp.float32)
    return hi, mid, lo


def _inproj_kernel(x_ref, ng_ref, wn_ref, wt_ref, bf_ref, cfdw_ref, cfb_ref, lng_ref, lnb_ref,
                   cfpw_ref, scdw_ref, gq_ref, gk_ref,
                   ycs_ref, qa_ref, ka_ref, va_ref, g_ref,
                   ubuf, abuf, wbuf, ccarry, *, ts):
    j = pl.program_id(1)

    @pl.when(j == 0)
    def _():
        ubuf[0:CF_HALO, :] = jnp.zeros((CF_HALO, D_CF), jnp.float32)
        wbuf[0:SC_HALO, :] = jnp.zeros((SC_HALO, D_SC), jnp.float32)
        ccarry[...] = jnp.zeros_like(ccarry)

    xf = x_ref[0]
    ms = jnp.mean(xf * xf, axis=-1, keepdims=True)
    hb = (xf * lax.rsqrt(ms + EPS) * ng_ref[...]).astype(jnp.bfloat16)

    p1 = jnp.dot(hb, wn_ref[:, 0:2 * D_CF], preferred_element_type=jnp.float32)
    ubuf[CF_HALO:CF_HALO + ts, :] = p1[:, 0:D_CF] * _sigmoid(p1[:, D_CF:2 * D_CF])

    p2 = jnp.dot(hb, wn_ref[:, 3 * D_CF:N_NAT], preferred_element_type=jnp.float32)
    sc_b = p2[:, 0:D_SC]
    sc_c = p2[:, D_SC:2 * D_SC]
    sc_x = p2[:, 2 * D_SC:3 * D_SC]
    sc_z = p2[:, 3 * D_SC:4 * D_SC]
    wbuf[SC_HALO:SC_HALO + ts, :] = sc_c * sc_x
    sconv = jnp.zeros((ts, D_SC), jnp.float32)
    for k in range(SC_WIDTH):
        off = SC_HALO - (SC_WIDTH - 1) + k
        sconv = sconv + scdw_ref[k:k + 1, :] * wbuf[off:off + ts, :]
    wbuf[0:SC_HALO, :] = wbuf[ts:ts + SC_HALO, :]
    ycs_ref[0, :, D_CF:D_CF + D_SC] = (sc_b * sconv * _silu(sc_z)).astype(ycs_ref.dtype)

    nt = (((1,), (1,)), ((), ()))

    zf = lax.dot_general(wt_ref[3 * D_ATT:N_T, :], hb, nt, preferred_element_type=jnp.float32)
    zt = zf[0:D_ATT]
    z = zf[D_ATT:D_ATT + N_HEADS] + bf_ref[...]
    c = jnp.minimum(z, 0.0) - jnp.log1p(jnp.exp(-jnp.abs(z)))
    lane = lax.broadcasted_iota(jnp.int32, (N_HEADS, ts), 1)
    d = 1
    while d < ts:
        c = c + jnp.where(lane >= d, pltpu.roll(c, d, axis=1), 0.0)
        d *= 2
    c = c + ccarry[:, 0:1]
    ccarry[...] = jnp.broadcast_to(c[:, ts - 1:ts], ccarry.shape)
    c = c * LOG2E

    row16 = lax.broadcasted_iota(jnp.int32, (16, ts), 0)
    zpad = jnp.zeros((K_AUG - HEAD_DIM - 16, ts), jnp.float32)

    def head_norm(t, g_ref):
        t3 = t.reshape(N_HEADS, HEAD_DIM, ts)
        ssq = jnp.mean(t3 * t3, axis=1, keepdims=True)
        return (t3 * lax.rsqrt(ssq + EPS)).reshape(D_ATT, ts) * g_ref[...]

    qn = head_norm(lax.dot_general(wt_ref[0:D_ATT, :], hb, nt,
                                   preferred_element_type=jnp.float32), gq_ref)
    kn = head_norm(lax.dot_general(wt_ref[D_ATT:2 * D_ATT, :], hb, nt,
                                   preferred_element_type=jnp.float32), gk_ref)
    vt = lax.dot_general(wt_ref[2 * D_ATT:3 * D_ATT, :], hb, nt,
                         preferred_element_type=jnp.float32)
    g_ref[0] = _silu(zt).astype(g_ref.dtype).reshape(N_HEADS, HEAD_DIM, ts)

    ones_row = jnp.where(
        lax.broadcasted_iota(jnp.int32, (V_AUG - HEAD_DIM, ts), 0) == 0, 1.0, 0.0)
    row8 = lax.broadcasted_iota(jnp.int32, (N_HEADS, ts), 0)
    for h in range(N_HEADS):
        c_h = jnp.sum(jnp.where(row8 == h, c, 0.0), axis=0, keepdims=True)
        hi, mid, lo = _split3(c_h)
        csplit = jnp.where(row16 == 0, hi, jnp.where(row16 == 1, mid, lo))
        q_ext = jnp.where(row16 < 3, csplit, jnp.where(row16 < 6, 1.0, 0.0))
        row_m3 = jnp.where(row16 == 3, hi, jnp.where(row16 == 4, mid, lo))
        k_ext = jnp.where(row16 < 3, 1.0, jnp.where(row16 < 6, -row_m3, 0.0))
        lo_h, hi_h = h * HEAD_DIM, (h + 1) * HEAD_DIM
        qa_ref[0, h, 0:HEAD_DIM, :] = qn[lo_h:hi_h].astype(qa_ref.dtype)
        qa_ref[0, h, HEAD_DIM:HEAD_DIM + 16, :] = q_ext.astype(qa_ref.dtype)
        qa_ref[0, h, HEAD_DIM + 16:K_AUG, :] = zpad.astype(qa_ref.dtype)
        k_aug_t = jnp.concatenate([kn[lo_h:hi_h], k_ext, zpad], axis=0)
        ka_ref[0, h] = k_aug_t.T.astype(ka_ref.dtype)
        va_ref[0, h, 0:HEAD_DIM, :] = vt[lo_h:hi_h].astype(va_ref.dtype)
        va_ref[0, h, HEAD_DIM:V_AUG, :] = ones_row.astype(va_ref.dtype)

    conv = jnp.broadcast_to(cfb_ref[...], (ts, D_CF))
    base = CF_HALO - (CF_WIDTH - 1)
    for r in range(8):
        taps = [k for k in range(CF_WIDTH) if (base + k) % 8 == r]
        if not taps:
            continue
        a_lo = (base + taps[0]) // 8
        a_hi = (base + taps[-1]) // 8
        rows = 8 * (a_hi - a_lo) + ts
        abuf[0:rows, :] = ubuf[8 * a_lo + r:8 * a_lo + r + rows, :]
        for k in taps:
            a = (base + k) // 8 - a_lo
            conv = conv + cfdw_ref[k:k + 1, :] * abuf[8 * a:8 * a + ts, :]
    ubuf[0:CF_HALO, :] = ubuf[ts:ts + CF_HALO, :]
    mu = jnp.mean(conv, axis=-1, keepdims=True)
    xc = conv - mu
    var = jnp.mean(xc * xc, axis=-1, keepdims=True)
    ln = xc * lax.rsqrt(var + EPS) * lng_ref[...] + lnb_ref[...]
    cf_z = jnp.dot(hb, wn_ref[:, 2 * D_CF:3 * D_CF], preferred_element_type=jnp.float32)
    y_cf = jnp.dot(_silu(ln).astype(jnp.bfloat16), cfpw_ref[...],
                   preferred_element_type=jnp.float32) * _silu(cf_z)
    ycs_ref[0, :, 0:D_CF] = y_cf.astype(ycs_ref.dtype)


def _attn_kernel(qa_ref, ka_ref, va_ref, g_ref, y_ref, s_scr, m_scr, *, seq, tq):
    nq = seq // tq
    kpos = lax.broadcasted_iota(jnp.int32, (tq, tq), 0)
    qpos = lax.broadcasted_iota(jnp.int32, (tq, tq), 1)
    causal = kpos <= qpos

    def scores_chunk(h, i, buf, c, m):
        qa = qa_ref[0, h, :, i * tq:(i + 1) * tq]
        s = jnp.dot(ka_ref[0, h, c * tq:(c + 1) * tq, :], qa, preferred_element_type=jnp.float32)
        if c == i:
            s = jnp.where(causal, s, NEG)
        s_scr[buf, c * tq:(c + 1) * tq, :] = s
        mc = jnp.max(s, axis=0, keepdims=True)
        m = mc if m is None else jnp.maximum(m, mc)
        if c == i:
            m_scr[buf] = jnp.broadcast_to(m, (8, tq))
        return m

    def finish_chunk(h, buf, c, acc):
        p = jnp.exp2(s_scr[buf, c * tq:(c + 1) * tq, :] - m_scr[buf, 0:1, :]).astype(jnp.bfloat16)
        d = jnp.dot(va_ref[0, h, :, c * tq:(c + 1) * tq], p, preferred_element_type=jnp.float32)
        return d if acc is None else acc + d

    def pair(hs_cur, i_cur, slot_cur, hs_nxt, i_nxt, slot_nxt):
        ms = [None] * HEAD_GROUP
        accs = [None] * HEAD_GROUP
        for c in range(max(i_cur, i_nxt) + 1):
            for g in range(HEAD_GROUP):
                if c <= i_nxt:
                    ms[g] = scores_chunk(hs_nxt[g], i_nxt, slot_nxt * HEAD_GROUP + g, c, ms[g])
            for g in range(HEAD_GROUP):
                if c <= i_cur:
                    accs[g] = finish_chunk(hs_cur[g], slot_cur * HEAD_GROUP + g, c, accs[g])
        q0 = i_cur * tq
        for g in range(HEAD_GROUP):
            o = accs[g][0:HEAD_DIM] / accs[g][HEAD_DIM:HEAD_DIM + 1]
            gate = g_ref[0, hs_cur[g], :, q0:q0 + tq].astype(jnp.float32)
            y_ref[0, hs_cur[g], :, q0:q0 + tq] = (o * gate).astype(y_ref.dtype)

    order = list(range(nq - 1, -1, -1))
    for g in range(HEAD_GROUP):
        m0 = None
        for c in range(order[0] + 1):
            m0 = scores_chunk(g, order[0], g, c, m0)

    n_groups = N_HEADS // HEAD_GROUP

    def group(hg, carry, last=False):
        hs = [hg * HEAD_GROUP + g for g in range(HEAD_GROUP)]
        hs_next = [(hg + 1) * HEAD_GROUP + g for g in range(HEAD_GROUP)]
        for n, i in enumerate(order):
            if n + 1 < nq:
                pair(hs, i, n % 2, hs, order[n + 1], (n + 1) % 2)
            elif not last:
                pair(hs, i, n % 2, hs_next, order[0], (n + 1) % 2)
            else:
                pair(hs, i, n % 2, None, -1, None)
        return carry

    lax.fori_loop(0, n_groups - 1, group, 0)
    group(n_groups - 1, 0, last=True)


def _outproj_kernel(x_ref, ycs_ref, yt_ref, woa_ref, wob_ref, o_ref):
    tn = (((0,), (0,)), ((), ()))
    acc = jnp.dot(ycs_ref[0], woa_ref[...], preferred_element_type=jnp.float32)
    acc = acc + lax.dot_general(yt_ref[0], wob_ref[...], tn, preferred_element_type=jnp.float32)
    o_ref[0] = x_ref[0] + acc


def _const_spec(shape):
    return pl.BlockSpec(shape, lambda *_: (0,) * len(shape))


def _inproj_call(x, ng, wn, wt, bf, cfdw, cfb, lng, lnb, cfpw, scdw, gq, gk, *, ts):
    bsz, seq, _ = x.shape
    bf16 = jnp.bfloat16
    out_shape = (
        jax.ShapeDtypeStruct((bsz, seq, D_CF + D_SC), bf16),
        jax.ShapeDtypeStruct((bsz, N_HEADS, K_AUG, seq), bf16),
        jax.ShapeDtypeStruct((bsz, N_HEADS, seq, K_AUG), bf16),
        jax.ShapeDtypeStruct((bsz, N_HEADS, V_AUG, seq), bf16),
        jax.ShapeDtypeStruct((bsz, N_HEADS, HEAD_DIM, seq), bf16),
    )
    in_specs = [
        pl.BlockSpec((1, ts, D_MODEL), lambda b, j: (b, j, 0)),
        _const_spec(ng.shape), _const_spec(wn.shape), _const_spec(wt.shape), _const_spec(bf.shape),
        _const_spec(cfdw.shape), _const_spec(cfb.shape), _const_spec(lng.shape),
        _const_spec(lnb.shape), _const_spec(cfpw.shape), _const_spec(scdw.shape),
        _const_spec(gq.shape), _const_spec(gk.shape),
    ]
    out_specs = (
        pl.BlockSpec((1, ts, D_CF + D_SC), lambda b, j: (b, j, 0)),
        pl.BlockSpec((1, N_HEADS, K_AUG, ts), lambda b, j: (b, 0, 0, j)),
        pl.BlockSpec((1, N_HEADS, ts, K_AUG), lambda b, j: (b, 0, j, 0)),
        pl.BlockSpec((1, N_HEADS, V_AUG, ts), lambda b, j: (b, 0, 0, j)),
        pl.BlockSpec((1, N_HEADS, HEAD_DIM, ts), lambda b, j: (b, 0, 0, j)),
    )
    return pl.pallas_call(
        functools.partial(_inproj_kernel, ts=ts),
        grid=(bsz, seq // ts),
        in_specs=in_specs,
        out_specs=out_specs,
        out_shape=out_shape,
        scratch_shapes=[
            pltpu.VMEM((ts + CF_HALO, D_CF), jnp.float32),
            pltpu.VMEM((ts + CF_HALO, D_CF), jnp.float32),
            pltpu.VMEM((ts + SC_HALO, D_SC), jnp.float32),
            pltpu.VMEM((N_HEADS, 128), jnp.float32),
        ],
        compiler_params=pltpu.CompilerParams(
            dimension_semantics=("arbitrary", "arbitrary"), vmem_limit_bytes=VMEM_LIMIT),
        name="inproj",
    )(x, ng, wn, wt, bf, cfdw, cfb, lng, lnb, cfpw, scdw, gq, gk)


def _attn_call(qa, ka, va, g, *, tq):
    bsz, _, _, seq = qa.shape

    def whole(arr):
        return pl.BlockSpec((1,) + arr.shape[1:], lambda b: (b, 0, 0, 0))

    return pl.pallas_call(
        functools.partial(_attn_kernel, seq=seq, tq=tq),
        grid=(bsz,),
        in_specs=[whole(qa), whole(ka), whole(va), whole(g)],
        out_specs=pl.BlockSpec((1, N_HEADS, HEAD_DIM, seq), lambda b: (b, 0, 0, 0)),
        out_shape=jax.ShapeDtypeStruct((bsz, N_HEADS, HEAD_DIM, seq), jnp.bfloat16),
        scratch_shapes=[
            pltpu.VMEM((2 * HEAD_GROUP, seq, tq), jnp.float32),
            pltpu.VMEM((2 * HEAD_GROUP, 8, tq), jnp.float32),
        ],
        compiler_params=pltpu.CompilerParams(
            dimension_semantics=("arbitrary",), vmem_limit_bytes=VMEM_LIMIT),
        name="fox_attention",
    )(qa, ka, va, g)


def _outproj_call(x, ycs, yt, woa, wob, *, ts):
    bsz, seq, _ = x.shape
    return pl.pallas_call(
        _outproj_kernel,
        grid=(bsz, seq // ts),
        in_specs=[
            pl.BlockSpec((1, ts, D_MODEL), lambda b, j: (b, j, 0)),
            pl.BlockSpec((1, ts, D_CF + D_SC), lambda b, j: (b, j, 0)),
            pl.BlockSpec((1, D_ATT, ts), lambda b, j: (b, 0, j)),
            _const_spec(woa.shape), _const_spec(wob.shape),
        ],
        out_specs=pl.BlockSpec((1, ts, D_MODEL), lambda b, j: (b, j, 0)),
        out_shape=jax.ShapeDtypeStruct(x.shape, x.dtype),
        compiler_params=pltpu.CompilerParams(
            dimension_semantics=("arbitrary", "arbitrary"), vmem_limit_bytes=VMEM_LIMIT),
        name="outproj",
    )(x, ycs, yt, woa, wob)


def _layer(x, norm_g, w_in, b_f, cf_dw, cf_dw_b, cf_ln_g, cf_ln_b, cf_pw, sc_dw,
           q_norm_g, k_norm_g, w_out):
    seq = x.shape[1]
    ts = min(TS, seq)
    tq = min(TQ, seq)
    assert seq % ts == 0 and seq % tq == 0 and (seq // tq) % 2 == 0
    bf16 = jnp.bfloat16
    f32 = jnp.float32
    wn = w_in[:, :N_NAT].astype(bf16)
    wt = jnp.pad(w_in[:, N_NAT:].T, ((0, N_T - (4 * D_ATT + N_HEADS)), (0, 0))).astype(bf16)
    scale = LOG2E / (HEAD_DIM ** 0.5)
    gq = (q_norm_g.astype(f32) * scale).reshape(D_ATT, 1)
    gk = k_norm_g.astype(f32).reshape(D_ATT, 1)
    ycs, qa, ka, va, g = _inproj_call(
        x, norm_g.reshape(1, D_MODEL), wn, wt, b_f.reshape(N_HEADS, 1),
        cf_dw, cf_dw_b.reshape(1, D_CF), cf_ln_g.reshape(1, D_CF), cf_ln_b.reshape(1, D_CF),
        cf_pw.astype(bf16), sc_dw, gq, gk, ts=ts)
    yt = _attn_call(qa, ka, va, g, tq=tq).reshape(x.shape[0], D_ATT, seq)
    woa = w_out[:D_CF + D_SC].astype(bf16)
    wob = w_out[D_CF + D_SC:].astype(bf16)
    return _outproj_call(x, ycs, yt, woa, wob, ts=min(TS_OUT, seq))


def kernel(x, norm_g, w_in, b_f, cf_dw, cf_dw_b, cf_ln_g, cf_ln_b, cf_pw, sc_dw, q_norm_g, k_norm_g, w_out):
    for l in range(norm_g.shape[0]):
        x = _layer(x, norm_g[l], w_in[l], b_f[l], cf_dw[l], cf_dw_b[l], cf_ln_g[l], cf_ln_b[l],
                   cf_pw[l], sc_dw[l], q_norm_g[l], k_norm_g[l], w_out[l])
    return x
```

```python
import functools

import jax
import jax.numpy as jnp
from jax import lax
from jax.experimental import pallas as pl
from jax.experimental.pallas import tpu as pltpu

D_MODEL = 1024
D_CF = 256
D_SC = 256
D_ATT = 512
HEAD_DIM = 64
N_HEADS = D_ATT // HEAD_DIM
CF_WIDTH = 31
SC_WIDTH = 3
EPS = 1e-6
N_NAT = 3 * D_CF + 4 * D_SC
N_T = 4 * D_ATT + 16
N_ALL = N_NAT + N_T

K_AUG = 128
V_AUG = 80
CF_HALO = 32
SC_HALO = 8
NEG = -1e30
LOG2E = 1.4426950408889634

TS = 512
TS_OUT = 2048
TQ = 256
HEAD_GROUP = 2
VMEM_LIMIT = 56 * 1024 * 1024


def _sigmoid(x):
    return 0.5 * jnp.tanh(0.5 * x) + 0.5


def _silu(x):
    hx = 0.5 * x
    return hx * jnp.tanh(hx) + hx


def _split3(c):
    hi = c.astype(jnp.bfloat16).astype(jnp.float32)
    r = c - hi
    mid = r.astype(jnp.bfloat16).astype(jnp.float32)
    lo = (r - mid).astype(jnp.bfloat16).astype(jnp.float32)
    return hi, mid, lo


def _inproj_kernel(x_ref, ng_ref, w_ref, bf_ref, cfdw_ref, cfb_ref, lng_ref, lnb_ref,
                   cfpw_ref, scdw_ref, gq_ref, gk_ref,
                   ycs_ref, qa_ref, ka_ref, va_ref, g_ref,
                   ubuf, abuf, wbuf, ccarry, *, ts):
    j = pl.program_id(1)
    (ng_ref, w_ref, bf_ref, cfdw_ref, cfb_ref, lng_ref, lnb_ref, cfpw_ref, scdw_ref, gq_ref,
     gk_ref) = (r.at[0] for r in (ng_ref, w_ref, bf_ref, cfdw_ref, cfb_ref, lng_ref, lnb_ref,
                                  cfpw_ref, scdw_ref, gq_ref, gk_ref))
    nt = (((1,), (1,)), ((), ()))

    def proj(lo, hi):
        return lax.dot_general(hb, w_ref[lo:hi, :], nt, preferred_element_type=jnp.float32)

    def proj_t(lo, hi):
        return lax.dot_general(w_ref[lo:hi, :], hb, nt, preferred_element_type=jnp.float32)

    @pl.when(j == 0)
    def _():
        ubuf[0:CF_HALO, :] = jnp.zeros((CF_HALO, D_CF), jnp.float32)
        wbuf[0:SC_HALO, :] = jnp.zeros((SC_HALO, D_SC), jnp.float32)
        ccarry[...] = jnp.zeros_like(ccarry)

    xf = x_ref[0]
    ms = jnp.mean(xf * xf, axis=-1, keepdims=True)
    hb = (xf * lax.rsqrt(ms + EPS) * ng_ref[...]).astype(jnp.bfloat16)

    p1 = proj(0, 3 * D_CF)
    ubuf[CF_HALO:CF_HALO + ts, :] = p1[:, 0:D_CF] * _sigmoid(p1[:, D_CF:2 * D_CF])
    cf_z = p1[:, 2 * D_CF:3 * D_CF]

    p2 = proj(3 * D_CF, N_NAT)
    sc_b = p2[:, 0:D_SC]
    sc_c = p2[:, D_SC:2 * D_SC]
    sc_x = p2[:, 2 * D_SC:3 * D_SC]
    sc_z = p2[:, 3 * D_SC:4 * D_SC]
    wbuf[SC_HALO:SC_HALO + ts, :] = sc_c * sc_x
    sconv = jnp.zeros((ts, D_SC), jnp.float32)
    for k in range(SC_WIDTH):
        off = SC_HALO - (SC_WIDTH - 1) + k
        sconv = sconv + scdw_ref[k:k + 1, :] * wbuf[off:off + ts, :]
    wbuf[0:SC_HALO, :] = wbuf[ts:ts + SC_HALO, :]
    ycs_ref[0, :, D_CF:D_CF + D_SC] = (sc_b * sconv * _silu(sc_z)).astype(ycs_ref.dtype)

    zf = proj_t(N_NAT + 3 * D_ATT, N_ALL)
    zt = zf[0:D_ATT]
    z = zf[D_ATT:D_ATT + N_HEADS] + bf_ref[...]
    c = jnp.minimum(z, 0.0) - jnp.log1p(jnp.exp(-jnp.abs(z)))
    lane = lax.broadcasted_iota(jnp.int32, (N_HEADS, ts), 1)
    d = 1
    while d < ts:
        c = c + jnp.where(lane >= d, pltpu.roll(c, d, axis=1), 0.0)
        d *= 2
    c = c + ccarry[:, 0:1]
    ccarry[...] = jnp.broadcast_to(c[:, ts - 1:ts], ccarry.shape)
    c = c * LOG2E

    row16 = lax.broadcasted_iota(jnp.int32, (16, ts), 0)
    zpad = jnp.zeros((K_AUG - HEAD_DIM - 16, ts), jnp.float32)

    def head_norm(t, g_ref):
        t3 = t.reshape(N_HEADS, HEAD_DIM, ts)
        ssq = jnp.mean(t3 * t3, axis=1, keepdims=True)
        return (t3 * lax.rsqrt(ssq + EPS)).reshape(D_ATT, ts) * g_ref[...]

    qn = head_norm(proj_t(N_NAT, N_NAT + D_ATT), gq_ref)
    kn = head_norm(proj_t(N_NAT + D_ATT, N_NAT + 2 * D_ATT), gk_ref)
    vt = proj_t(N_NAT + 2 * D_ATT, N_NAT + 3 * D_ATT)
    g_ref[0] = _silu(zt).astype(g_ref.dtype).reshape(N_HEADS, HEAD_DIM, ts)

    ones_row = jnp.where(
        lax.broadcasted_iota(jnp.int32, (V_AUG - HEAD_DIM, ts), 0) == 0, 1.0, 0.0)
    row8 = lax.broadcasted_iota(jnp.int32, (N_HEADS, ts), 0)
    for h in range(N_HEADS):
        c_h = jnp.sum(jnp.where(row8 == h, c, 0.0), axis=0, keepdims=True)
        hi, mid, lo = _split3(c_h)
        csplit = jnp.where(row16 == 0, hi, jnp.where(row16 == 1, mid, lo))
        q_ext = jnp.where(row16 < 3, csplit, jnp.where(row16 < 6, 1.0, 0.0))
        row_m3 = jnp.where(row16 == 3, hi, jnp.where(row16 == 4, mid, lo))
        k_ext = jnp.where(row16 < 3, 1.0, jnp.where(row16 < 6, -row_m3, 0.0))
        lo_h, hi_h = h * HEAD_DIM, (h + 1) * HEAD_DIM
        qa_ref[0, h, 0:HEAD_DIM, :] = qn[lo_h:hi_h].astype(qa_ref.dtype)
        qa_ref[0, h, HEAD_DIM:HEAD_DIM + 16, :] = q_ext.astype(qa_ref.dtype)
        qa_ref[0, h, HEAD_DIM + 16:K_AUG, :] = zpad.astype(qa_ref.dtype)
        k_aug_t = jnp.concatenate([kn[lo_h:hi_h], k_ext, zpad], axis=0)
        ka_ref[0, h] = k_aug_t.T.astype(ka_ref.dtype)
        va_ref[0, h, 0:HEAD_DIM, :] = vt[lo_h:hi_h].astype(va_ref.dtype)
        va_ref[0, h, HEAD_DIM:V_AUG, :] = ones_row.astype(va_ref.dtype)

    conv = jnp.broadcast_to(cfb_ref[...], (ts, D_CF))
    base = CF_HALO - (CF_WIDTH - 1)
    for r in range(8):
        taps = [k for k in range(CF_WIDTH) if (base + k) % 8 == r]
        if not taps:
            continue
        a_lo = (base + taps[0]) // 8
        a_hi = (base + taps[-1]) // 8
        rows = 8 * (a_hi - a_lo) + ts
        abuf[0:rows, :] = ubuf[8 * a_lo + r:8 * a_lo + r + rows, :]
        for k in taps:
            a = (base + k) // 8 - a_lo
            conv = conv + cfdw_ref[k:k + 1, :] * abuf[8 * a:8 * a + ts, :]
    ubuf[0:CF_HALO, :] = ubuf[ts:ts + CF_HALO, :]
    mu = jnp.mean(conv, axis=-1, keepdims=True)
    xc = conv - mu
    var = jnp.mean(xc * xc, axis=-1, keepdims=True)
    ln = xc * lax.rsqrt(var + EPS) * lng_ref[...] + lnb_ref[...]
    y_cf = jnp.dot(_silu(ln).astype(jnp.bfloat16), cfpw_ref[...],
                   preferred_element_type=jnp.float32) * _silu(cf_z)
    ycs_ref[0, :, 0:D_CF] = y_cf.astype(ycs_ref.dtype)


def _attn_kernel(qa_ref, ka_ref, va_ref, g_ref, y_ref, s_scr, m_scr, *, seq, tq):
    nq = seq // tq
    kpos = lax.broadcasted_iota(jnp.int32, (tq, tq), 0)
    qpos = lax.broadcasted_iota(jnp.int32, (tq, tq), 1)
    causal = kpos <= qpos

    def scores_chunk(h, i, buf, c, m):
        qa = qa_ref[0, h, :, i * tq:(i + 1) * tq]
        s = jnp.dot(ka_ref[0, h, c * tq:(c + 1) * tq, :], qa, preferred_element_type=jnp.float32)
        if c == i:
            s = jnp.where(causal, s, NEG)
        s_scr[buf, c * tq:(c + 1) * tq, :] = s
        mc = jnp.max(s, axis=0, keepdims=True)
        m = mc if m is None else jnp.maximum(m, mc)
        if c == i:
            m_scr[buf] = jnp.broadcast_to(m, (8, tq))
        return m

    def finish_chunk(h, buf, c, acc):
        p = jnp.exp2(s_scr[buf, c * tq:(c + 1) * tq, :] - m_scr[buf, 0:1, :]).astype(jnp.bfloat16)
        d = jnp.dot(va_ref[0, h, :, c * tq:(c + 1) * tq], p, preferred_element_type=jnp.float32)
        return d if acc is None else acc + d

    def pair(hs_cur, i_cur, slot_cur, hs_nxt, i_nxt, slot_nxt):
        ms = [None] * HEAD_GROUP
        accs = [None] * HEAD_GROUP
        for c in range(max(i_cur, i_nxt) + 1):
            for g in range(HEAD_GROUP):
                if c <= i_nxt:
                    ms[g] = scores_chunk(hs_nxt[g], i_nxt, slot_nxt * HEAD_GROUP + g, c, ms[g])
            for g in range(HEAD_GROUP):
                if c <= i_cur:
                    accs[g] = finish_chunk(hs_cur[g], slot_cur * HEAD_GROUP + g, c, accs[g])
        q0 = i_cur * tq
        for g in range(HEAD_GROUP):
            o = accs[g][0:HEAD_DIM] / accs[g][HEAD_DIM:HEAD_DIM + 1]
            gate = g_ref[0, hs_cur[g], :, q0:q0 + tq].astype(jnp.float32)
            y_ref[0, hs_cur[g], :, q0:q0 + tq] = (o * gate).astype(y_ref.dtype)

    order = list(range(nq - 1, -1, -1))
    for g in range(HEAD_GROUP):
        m0 = None
        for c in range(order[0] + 1):
            m0 = scores_chunk(g, order[0], g, c, m0)

    n_groups = N_HEADS // HEAD_GROUP

    def group(hg, carry, last=False):
        hs = [hg * HEAD_GROUP + g for g in range(HEAD_GROUP)]
        hs_next = [(hg + 1) * HEAD_GROUP + g for g in range(HEAD_GROUP)]
        for n, i in enumerate(order):
            if n + 1 < nq:
                pair(hs, i, n % 2, hs, order[n + 1], (n + 1) % 2)
            elif not last:
                pair(hs, i, n % 2, hs_next, order[0], (n + 1) % 2)
            else:
                pair(hs, i, n % 2, None, -1, None)
        return carry

    lax.fori_loop(0, n_groups - 1, group, 0)
    group(n_groups - 1, 0, last=True)


def _outproj_kernel(x_ref, ycs_ref, yt_ref, woa_ref, wob_ref, o_ref):
    tn = (((0,), (0,)), ((), ()))
    acc = jnp.dot(ycs_ref[0], woa_ref[0], preferred_element_type=jnp.float32)
    acc = acc + lax.dot_general(yt_ref[0], wob_ref[0], tn, preferred_element_type=jnp.float32)
    o_ref[0] = x_ref[0] + acc


def _layer_spec(arr, layer):
    zeros = (0,) * (arr.ndim - 1)
    return pl.BlockSpec((1,) + arr.shape[1:], lambda *_: (layer,) + zeros)


def _inproj_call(x, layer, params, *, ts):
    bsz, seq, _ = x.shape
    bf16 = jnp.bfloat16
    out_shape = (
        jax.ShapeDtypeStruct((bsz, seq, D_CF + D_SC), bf16),
        jax.ShapeDtypeStruct((bsz, N_HEADS, K_AUG, seq), bf16),
        jax.ShapeDtypeStruct((bsz, N_HEADS, seq, K_AUG), bf16),
        jax.ShapeDtypeStruct((bsz, N_HEADS, V_AUG, seq), bf16),
        jax.ShapeDtypeStruct((bsz, N_HEADS, HEAD_DIM, seq), bf16),
    )
    in_specs = [pl.BlockSpec((1, ts, D_MODEL), lambda b, j: (b, j, 0))]
    in_specs += [_layer_spec(p, layer) for p in params]
    out_specs = (
        pl.BlockSpec((1, ts, D_CF + D_SC), lambda b, j: (b, j, 0)),
        pl.BlockSpec((1, N_HEADS, K_AUG, ts), lambda b, j: (b, 0, 0, j)),
        pl.BlockSpec((1, N_HEADS, ts, K_AUG), lambda b, j: (b, 0, j, 0)),
        pl.BlockSpec((1, N_HEADS, V_AUG, ts), lambda b, j: (b, 0, 0, j)),
        pl.BlockSpec((1, N_HEADS, HEAD_DIM, ts), lambda b, j: (b, 0, 0, j)),
    )
    return pl.pallas_call(
        functools.partial(_inproj_kernel, ts=ts),
        grid=(bsz, seq // ts),
        in_specs=in_specs,
        out_specs=out_specs,
        out_shape=out_shape,
        scratch_shapes=[
            pltpu.VMEM((ts + CF_HALO, D_CF), jnp.float32),
            pltpu.VMEM((ts + CF_HALO, D_CF), jnp.float32),
            pltpu.VMEM((ts + SC_HALO, D_SC), jnp.float32),
            pltpu.VMEM((N_HEADS, 128), jnp.float32),
        ],
        compiler_params=pltpu.CompilerParams(
            dimension_semantics=("arbitrary", "arbitrary"), vmem_limit_bytes=VMEM_LIMIT),
        name="inproj",
    )(x, *params)


def _attn_call(qa, ka, va, g, *, tq):
    bsz, _, _, seq = qa.shape

    def whole(arr):
        return pl.BlockSpec((1,) + arr.shape[1:], lambda b: (b, 0, 0, 0))

    return pl.pallas_call(
        functools.partial(_attn_kernel, seq=seq, tq=tq),
        grid=(bsz,),
        in_specs=[whole(qa), whole(ka), whole(va), whole(g)],
        out_specs=pl.BlockSpec((1, N_HEADS, HEAD_DIM, seq), lambda b: (b, 0, 0, 0)),
        out_shape=jax.ShapeDtypeStruct((bsz, N_HEADS, HEAD_DIM, seq), jnp.bfloat16),
        scratch_shapes=[
            pltpu.VMEM((2 * HEAD_GROUP, seq, tq), jnp.float32),
            pltpu.VMEM((2 * HEAD_GROUP, 8, tq), jnp.float32),
        ],
        compiler_params=pltpu.CompilerParams(
            dimension_semantics=("arbitrary",), vmem_limit_bytes=VMEM_LIMIT),
        name="fox_attention",
    )(qa, ka, va, g)


def _outproj_call(x, ycs, yt, wo, layer, *, ts):
    bsz, seq, _ = x.shape
    d_nat = D_CF + D_SC
    return pl.pallas_call(
        _outproj_kernel,
        grid=(bsz, seq // ts),
        in_specs=[
            pl.BlockSpec((1, ts, D_MODEL), lambda b, j: (b, j, 0)),
            pl.BlockSpec((1, ts, d_nat), lambda b, j: (b, j, 0)),
            pl.BlockSpec((1, D_ATT, ts), lambda b, j: (b, 0, j)),
            pl.BlockSpec((1, d_nat, D_MODEL), lambda b, j: (layer, 0, 0)),
            pl.BlockSpec((1, D_ATT, D_MODEL), lambda b, j: (layer, 1, 0)),
        ],
        out_specs=pl.BlockSpec((1, ts, D_MODEL), lambda b, j: (b, j, 0)),
        out_shape=jax.ShapeDtypeStruct(x.shape, x.dtype),
        compiler_params=pltpu.CompilerParams(
            dimension_semantics=("arbitrary", "arbitrary"), vmem_limit_bytes=VMEM_LIMIT),
        name="outproj",
    )(x, ycs, yt, wo, wo)


def kernel(x, norm_g, w_in, b_f, cf_dw, cf_dw_b, cf_ln_g, cf_ln_b, cf_pw, sc_dw, q_norm_g, k_norm_g, w_out):
    depth = norm_g.shape[0]
    seq = x.shape[1]
    ts = min(TS, seq)
    tq = min(TQ, seq)
    assert seq % ts == 0 and seq % tq == 0 and (seq // tq) % 2 == 0
    assert D_CF + D_SC == D_ATT
    bf16 = jnp.bfloat16
    f32 = jnp.float32
    w_all = jnp.pad(jnp.swapaxes(w_in, 1, 2), ((0, 0), (0, N_ALL - w_in.shape[2]), (0, 0))).astype(bf16)
    scale = LOG2E / (HEAD_DIM ** 0.5)
    params = (
        norm_g.reshape(depth, 1, D_MODEL), w_all, b_f.reshape(depth, N_HEADS, 1), cf_dw,
        cf_dw_b.reshape(depth, 1, D_CF), cf_ln_g.reshape(depth, 1, D_CF), cf_ln_b.reshape(depth, 1, D_CF),
        cf_pw.astype(bf16), sc_dw, (q_norm_g.astype(f32) * scale).reshape(depth, D_ATT, 1),
        k_norm_g.astype(f32).reshape(depth, D_ATT, 1),
    )
    wo = w_out.astype(bf16)
    for layer in range(depth):
        ycs, qa, ka, va, g = _inproj_call(x, layer, params, ts=ts)
        yt = _attn_call(qa, ka, va, g, tq=tq).reshape(x.shape[0], D_ATT, seq)
        x = _outproj_call(x, ycs, yt, wo, layer, ts=min(TS_OUT, seq))
    return x
```

```python
import functools

import jax
import jax.numpy as jnp
from jax import lax
from jax.experimental import pallas as pl
from jax.experimental.pallas import tpu as pltpu

D_MODEL = 1024
D_CF = 256
D_SC = 256
D_ATT = 512
HEAD_DIM = 64
N_HEADS = D_ATT // HEAD_DIM
CF_WIDTH = 31
SC_WIDTH = 3
EPS = 1e-6
N_NAT = 3 * D_CF + 4 * D_SC
N_T = 4 * D_ATT + N_HEADS
N_ALL = N_NAT + N_T

K_AUG = 128
V_AUG = 80
CF_HALO = 32
SC_HALO = 8
NEG = -1e30
LOG2E = 1.4426950408889634

TS = 1024
TS_OUT = 2048
TQ = 256
HEAD_GROUP = 2
VMEM_LIMIT = 56 * 1024 * 1024


def _sigmoid(x):
    return 0.5 * jnp.tanh(0.5 * x) + 0.5


def _silu(x):
    hx = 0.5 * x
    return hx * jnp.tanh(hx) + hx


def _split3(c):
    hi = c.astype(jnp.bfloat16).astype(jnp.float32)
    r = c - hi
    mid = r.astype(jnp.bfloat16).astype(jnp.float32)
    lo = (r - mid).astype(jnp.bfloat16).astype(jnp.float32)
    return hi, mid, lo


def _inproj_kernel(x_ref, ng_ref, w_ref, bf_ref, cfdw_ref, cfb_ref, lng_ref, lnb_ref,
                   cfpw_ref, scdw_ref, gq_ref, gk_ref,
                   ycs_ref, qa_ref, ka_ref, va_ref, g_ref,
                   ubuf, abuf, wbuf, ccarry, *, ts):
    j = pl.program_id(1)
    (ng_ref, w_ref, bf_ref, cfdw_ref, cfb_ref, lng_ref, lnb_ref, cfpw_ref, scdw_ref, gq_ref,
     gk_ref) = (r.at[0] for r in (ng_ref, w_ref, bf_ref, cfdw_ref, cfb_ref, lng_ref, lnb_ref,
                                  cfpw_ref, scdw_ref, gq_ref, gk_ref))
    nt = (((1,), (1,)), ((), ()))

    def proj(lo, hi):
        return lax.dot_general(hb, w_ref[lo:hi, :], nt, preferred_element_type=jnp.float32)

    def proj_t(lo, hi):
        return lax.dot_general(w_ref[lo:hi, :], hb, nt, preferred_element_type=jnp.float32)

    @pl.when(j == 0)
    def _():
        ubuf[0:CF_HALO, :] = jnp.zeros((CF_HALO, D_CF), jnp.float32)
        wbuf[0:SC_HALO, :] = jnp.zeros((SC_HALO, D_SC), jnp.float32)
        ccarry[...] = jnp.zeros_like(ccarry)

    xf = x_ref[0]
    ms = jnp.mean(xf * xf, axis=-1, keepdims=True)
    hb = (xf * lax.rsqrt(ms + EPS) * ng_ref[...]).astype(jnp.bfloat16)

    p1 = proj(0, 3 * D_CF)
    ubuf[CF_HALO:CF_HALO + ts, :] = p1[:, 0:D_CF] * _sigmoid(p1[:, D_CF:2 * D_CF])
    cf_z = p1[:, 2 * D_CF:3 * D_CF]

    p2 = proj(3 * D_CF, N_NAT)
    sc_b = p2[:, 0:D_SC]
    sc_c = p2[:, D_SC:2 * D_SC]
    sc_x = p2[:, 2 * D_SC:3 * D_SC]
    sc_z = p2[:, 3 * D_SC:4 * D_SC]
    wbuf[SC_HALO:SC_HALO + ts, :] = sc_c * sc_x
    sconv = jnp.zeros((ts, D_SC), jnp.float32)
    for k in range(SC_WIDTH):
        off = SC_HALO - (SC_WIDTH - 1) + k
        sconv = sconv + scdw_ref[k:k + 1, :] * wbuf[off:off + ts, :]
    wbuf[0:SC_HALO, :] = wbuf[ts:ts + SC_HALO, :]
    ycs_ref[0, :, D_CF:D_CF + D_SC] = (sc_b * sconv * _silu(sc_z)).astype(ycs_ref.dtype)

    zf = proj_t(N_NAT + 3 * D_ATT, N_ALL)
    zt = zf[0:D_ATT]
    z = zf[D_ATT:D_ATT + N_HEADS] + bf_ref[...]
    c = jnp.minimum(z, 0.0) - jnp.log1p(jnp.exp(-jnp.abs(z)))
    lane = lax.broadcasted_iota(jnp.int32, (N_HEADS, ts), 1)
    d = 1
    while d < ts:
        c = c + jnp.where(lane >= d, pltpu.roll(c, d, axis=1), 0.0)
        d *= 2
    c = c + ccarry[:, 0:1]
    ccarry[...] = jnp.broadcast_to(c[:, ts - 1:ts], ccarry.shape)
    c = c * LOG2E

    row16 = lax.broadcasted_iota(jnp.int32, (16, ts), 0)
    zpad = jnp.zeros((K_AUG - HEAD_DIM - 16, ts), jnp.float32)

    def head_norm(t, g_ref):
        t3 = t.reshape(N_HEADS, HEAD_DIM, ts)
        ssq = jnp.mean(t3 * t3, axis=1, keepdims=True)
        return (t3 * lax.rsqrt(ssq + EPS)).reshape(D_ATT, ts) * g_ref[...]

    qn = head_norm(proj_t(N_NAT, N_NAT + D_ATT), gq_ref)
    kn = head_norm(proj_t(N_NAT + D_ATT, N_NAT + 2 * D_ATT), gk_ref)
    vt = proj_t(N_NAT + 2 * D_ATT, N_NAT + 3 * D_ATT)
    g_ref[0] = _silu(zt).astype(g_ref.dtype).reshape(N_HEADS, HEAD_DIM, ts)

    ones_row = jnp.where(
        lax.broadcasted_iota(jnp.int32, (V_AUG - HEAD_DIM, ts), 0) == 0, 1.0, 0.0)
    row8 = lax.broadcasted_iota(jnp.int32, (N_HEADS, ts), 0)
    for h in range(N_HEADS):
        c_h = jnp.sum(jnp.where(row8 == h, c, 0.0), axis=0, keepdims=True)
        hi, mid, lo = _split3(c_h)
        csplit = jnp.where(row16 == 0, hi, jnp.where(row16 == 1, mid, lo))
        q_ext = jnp.where(row16 < 3, csplit, jnp.where(row16 < 6, 1.0, 0.0))
        row_m3 = jnp.where(row16 == 3, hi, jnp.where(row16 == 4, mid, lo))
        k_ext = jnp.where(row16 < 3, 1.0, jnp.where(row16 < 6, -row_m3, 0.0))
        lo_h, hi_h = h * HEAD_DIM, (h + 1) * HEAD_DIM
        qa_ref[0, h, 0:HEAD_DIM, :] = qn[lo_h:hi_h].astype(qa_ref.dtype)
        qa_ref[0, h, HEAD_DIM:HEAD_DIM + 16, :] = q_ext.astype(qa_ref.dtype)
        qa_ref[0, h, HEAD_DIM + 16:K_AUG, :] = zpad.astype(qa_ref.dtype)
        k_aug_t = jnp.concatenate([kn[lo_h:hi_h], k_ext, zpad], axis=0)
        ka_ref[0, h] = k_aug_t.T.astype(ka_ref.dtype)
        va_ref[0, h, 0:HEAD_DIM, :] = vt[lo_h:hi_h].astype(va_ref.dtype)
        va_ref[0, h, HEAD_DIM:V_AUG, :] = ones_row.astype(va_ref.dtype)

    conv = jnp.broadcast_to(cfb_ref[...], (ts, D_CF))
    base = CF_HALO - (CF_WIDTH - 1)
    for r in range(8):
        taps = [k for k in range(CF_WIDTH) if (base + k) % 8 == r]
        if not taps:
            continue
        a_lo = (base + taps[0]) // 8
        a_hi = (base + taps[-1]) // 8
        rows = 8 * (a_hi - a_lo) + ts
        abuf[0:rows, :] = ubuf[8 * a_lo + r:8 * a_lo + r + rows, :]
        for k in taps:
            a = (base + k) // 8 - a_lo
            conv = conv + cfdw_ref[k:k + 1, :] * abuf[8 * a:8 * a + ts, :]
    ubuf[0:CF_HALO, :] = ubuf[ts:ts + CF_HALO, :]
    mu = jnp.mean(conv, axis=-1, keepdims=True)
    xc = conv - mu
    var = jnp.mean(xc * xc, axis=-1, keepdims=True)
    ln = xc * lax.rsqrt(var + EPS) * lng_ref[...] + lnb_ref[...]
    y_cf = jnp.dot(_silu(ln).astype(jnp.bfloat16), cfpw_ref[...],
                   preferred_element_type=jnp.float32) * _silu(cf_z)
    ycs_ref[0, :, 0:D_CF] = y_cf.astype(ycs_ref.dtype)


def _attn_kernel(qa_ref, ka_ref, va_ref, g_ref, y_ref, s_scr, m_scr, *, seq, tq):
    nq = seq // tq
    kpos = lax.broadcasted_iota(jnp.int32, (tq, tq), 0)
    qpos = lax.broadcasted_iota(jnp.int32, (tq, tq), 1)
    causal = kpos <= qpos

    def scores_chunk(h, i, buf, c, m):
        qa = qa_ref[0, h, :, i * tq:(i + 1) * tq]
        s = jnp.dot(ka_ref[0, h, c * tq:(c + 1) * tq, :], qa, preferred_element_type=jnp.float32)
        if c == i:
            s = jnp.where(causal, s, NEG)
        s_scr[buf, c * tq:(c + 1) * tq, :] = s
        mc = jnp.max(s, axis=0, keepdims=True)
        m = mc if m is None else jnp.maximum(m, mc)
        if c == i:
            m_scr[buf] = jnp.broadcast_to(m, (8, tq))
        return m

    def finish_chunk(h, buf, c, acc):
        p = jnp.exp2(s_scr[buf, c * tq:(c + 1) * tq, :] - m_scr[buf, 0:1, :]).astype(jnp.bfloat16)
        d = jnp.dot(va_ref[0, h, :, c * tq:(c + 1) * tq], p, preferred_element_type=jnp.float32)
        return d if acc is None else acc + d

    def pair(hs_cur, i_cur, slot_cur, hs_nxt, i_nxt, slot_nxt):
        ms = [None] * HEAD_GROUP
        accs = [None] * HEAD_GROUP
        for c in range(max(i_cur, i_nxt) + 1):
            for g in range(HEAD_GROUP):
                if c <= i_nxt:
                    ms[g] = scores_chunk(hs_nxt[g], i_nxt, slot_nxt * HEAD_GROUP + g, c, ms[g])
            for g in range(HEAD_GROUP):
                if c <= i_cur:
                    accs[g] = finish_chunk(hs_cur[g], slot_cur * HEAD_GROUP + g, c, accs[g])
        q0 = i_cur * tq
        for g in range(HEAD_GROUP):
            o = accs[g][0:HEAD_DIM] / accs[g][HEAD_DIM:HEAD_DIM + 1]
            gate = g_ref[0, hs_cur[g], :, q0:q0 + tq].astype(jnp.float32)
            y_ref[0, hs_cur[g], :, q0:q0 + tq] = (o * gate).astype(y_ref.dtype)

    order = list(range(nq - 1, -1, -1))
    for g in range(HEAD_GROUP):
        m0 = None
        for c in range(order[0] + 1):
            m0 = scores_chunk(g, order[0], g, c, m0)

    n_groups = N_HEADS // HEAD_GROUP

    def group(hg, carry, last=False):
        hs = [hg * HEAD_GROUP + g for g in range(HEAD_GROUP)]
        hs_next = [(hg + 1) * HEAD_GROUP + g for g in range(HEAD_GROUP)]
        for n, i in enumerate(order):
            if n + 1 < nq:
                pair(hs, i, n % 2, hs, order[n + 1], (n + 1) % 2)
            elif not last:
                pair(hs, i, n % 2, hs_next, order[0], (n + 1) % 2)
            else:
                pair(hs, i, n % 2, None, -1, None)
        return carry

    lax.fori_loop(0, n_groups - 1, group, 0)
    group(n_groups - 1, 0, last=True)


def _outproj_kernel(x_ref, ycs_ref, yt_ref, woa_ref, wob_ref, o_ref):
    tn = (((0,), (0,)), ((), ()))
    acc = jnp.dot(ycs_ref[0], woa_ref[0], preferred_element_type=jnp.float32)
    acc = acc + lax.dot_general(yt_ref[0], wob_ref[0], tn, preferred_element_type=jnp.float32)
    o_ref[0] = x_ref[0] + acc


def _layer_spec(arr, layer):
    zeros = (0,) * (arr.ndim - 1)
    return pl.BlockSpec((1,) + arr.shape[1:], lambda *_: (layer,) + zeros)


def _inproj_call(x, layer, params, *, ts):
    bsz, seq, _ = x.shape
    bf16 = jnp.bfloat16
    out_shape = (
        jax.ShapeDtypeStruct((bsz, seq, D_CF + D_SC), bf16),
        jax.ShapeDtypeStruct((bsz, N_HEADS, K_AUG, seq), bf16),
        jax.ShapeDtypeStruct((bsz, N_HEADS, seq, K_AUG), bf16),
        jax.ShapeDtypeStruct((bsz, N_HEADS, V_AUG, seq), bf16),
        jax.ShapeDtypeStruct((bsz, N_HEADS, HEAD_DIM, seq), bf16),
    )
    in_specs = [pl.BlockSpec((1, ts, D_MODEL), lambda b, j: (b, j, 0))]
    in_specs += [_layer_spec(p, layer) for p in params]
    out_specs = (
        pl.BlockSpec((1, ts, D_CF + D_SC), lambda b, j: (b, j, 0)),
        pl.BlockSpec((1, N_HEADS, K_AUG, ts), lambda b, j: (b, 0, 0, j)),
        pl.BlockSpec((1, N_HEADS, ts, K_AUG), lambda b, j: (b, 0, j, 0)),
        pl.BlockSpec((1, N_HEADS, V_AUG, ts), lambda b, j: (b, 0, 0, j)),
        pl.BlockSpec((1, N_HEADS, HEAD_DIM, ts), lambda b, j: (b, 0, 0, j)),
    )
    return pl.pallas_call(
        functools.partial(_inproj_kernel, ts=ts),
        grid=(bsz, seq // ts),
        in_specs=in_specs,
        out_specs=out_specs,
        out_shape=out_shape,
        scratch_shapes=[
            pltpu.VMEM((ts + CF_HALO, D_CF), jnp.float32),
            pltpu.VMEM((ts + CF_HALO, D_CF), jnp.float32),
            pltpu.VMEM((ts + SC_HALO, D_SC), jnp.float32),
            pltpu.VMEM((N_HEADS, 128), jnp.float32),
        ],
        compiler_params=pltpu.CompilerParams(
            dimension_semantics=("arbitrary", "arbitrary"), vmem_limit_bytes=VMEM_LIMIT),
        name="inproj",
    )(x, *params)


def _attn_call(qa, ka, va, g, *, tq):
    bsz, _, _, seq = qa.shape

    def whole(arr):
        return pl.BlockSpec((1,) + arr.shape[1:], lambda b: (b, 0, 0, 0))

    return pl.pallas_call(
        functools.partial(_attn_kernel, seq=seq, tq=tq),
        grid=(bsz,),
        in_specs=[whole(qa), whole(ka), whole(va), whole(g)],
        out_specs=pl.BlockSpec((1, N_HEADS, HEAD_DIM, seq), lambda b: (b, 0, 0, 0)),
        out_shape=jax.ShapeDtypeStruct((bsz, N_HEADS, HEAD_DIM, seq), jnp.bfloat16),
        scratch_shapes=[
            pltpu.VMEM((2 * HEAD_GROUP, seq, tq), jnp.float32),
            pltpu.VMEM((2 * HEAD_GROUP, 8, tq), jnp.float32),
        ],
        compiler_params=pltpu.CompilerParams(
            dimension_semantics=("arbitrary",), vmem_limit_bytes=VMEM_LIMIT),
        name="fox_attention",
    )(qa, ka, va, g)


def _outproj_call(x, ycs, yt, wo, layer, *, ts):
    bsz, seq, _ = x.shape
    d_nat = D_CF + D_SC
    return pl.pallas_call(
        _outproj_kernel,
        grid=(bsz, seq // ts),
        in_specs=[
            pl.BlockSpec((1, ts, D_MODEL), lambda b, j: (b, j, 0)),
            pl.BlockSpec((1, ts, d_nat), lambda b, j: (b, j, 0)),
            pl.BlockSpec((1, D_ATT, ts), lambda b, j: (b, 0, j)),
            pl.BlockSpec((1, d_nat, D_MODEL), lambda b, j: (layer, 0, 0)),
            pl.BlockSpec((1, D_ATT, D_MODEL), lambda b, j: (layer, 1, 0)),
        ],
        out_specs=pl.BlockSpec((1, ts, D_MODEL), lambda b, j: (b, j, 0)),
        out_shape=jax.ShapeDtypeStruct(x.shape, x.dtype),
        compiler_params=pltpu.CompilerParams(
            dimension_semantics=("arbitrary", "arbitrary"), vmem_limit_bytes=VMEM_LIMIT),
        name="outproj",
    )(x, ycs, yt, wo, wo)


def kernel(x, norm_g, w_in, b_f, cf_dw, cf_dw_b, cf_ln_g, cf_ln_b, cf_pw, sc_dw, q_norm_g, k_norm_g, w_out):
    depth = norm_g.shape[0]
    seq = x.shape[1]
    ts = min(TS, seq)
    tq = min(TQ, seq)
    assert seq % ts == 0 and seq % tq == 0 and (seq // tq) % 2 == 0
    assert D_CF + D_SC == D_ATT
    bf16 = jnp.bfloat16
    f32 = jnp.float32
    assert w_in.shape[2] == N_ALL
    w_all = jnp.swapaxes(w_in, 1, 2).astype(bf16)
    scale = LOG2E / (HEAD_DIM ** 0.5)
    params = (
        norm_g.reshape(depth, 1, D_MODEL), w_all, b_f.reshape(depth, N_HEADS, 1), cf_dw,
        cf_dw_b.reshape(depth, 1, D_CF), cf_ln_g.reshape(depth, 1, D_CF), cf_ln_b.reshape(depth, 1, D_CF),
        cf_pw.astype(bf16), sc_dw, (q_norm_g.astype(f32) * scale).reshape(depth, D_ATT, 1),
        k_norm_g.astype(f32).reshape(depth, D_ATT, 1),
    )
    wo = w_out.astype(bf16)
    for layer in range(depth):
        ycs, qa, ka, va, g = _inproj_call(x, layer, params, ts=ts)
        yt = _attn_call(qa, ka, va, g, tq=tq).reshape(x.shape[0], D_ATT, seq)
        x = _outproj_call(x, ycs, yt, wo, layer, ts=min(TS_OUT, seq))
    return x
```

```python
import functools

import jax
import jax.numpy as jnp
from jax import lax
from jax.experimental import pallas as pl
from jax.experimental.pallas import tpu as pltpu

D_MODEL = 1024
D_CF = 256
D_SC = 256
D_ATT = 512
HEAD_DIM = 64
N_HEADS = D_ATT // HEAD_DIM
CF_WIDTH = 31
SC_WIDTH = 3
EPS = 1e-6
N_NAT = 3 * D_CF + 4 * D_SC
N_T = 4 * D_ATT + N_HEADS
N_ALL = N_NAT + N_T

K_AUG = 128
V_AUG = 80
CF_HALO = 32
SC_HALO = 8
NEG = -1e30
LOG2E = 1.4426950408889634

TS = 1024
TS_OUT = 1024
TQ = 256
HEAD_GROUP = 2
VMEM_LIMIT = 56 * 1024 * 1024


def _sigmoid(x):
    return 0.5 * jnp.tanh(0.5 * x) + 0.5


def _silu(x):
    hx = 0.5 * x
    return hx * jnp.tanh(hx) + hx


def _split3(c):
    hi = c.astype(jnp.bfloat16).astype(jnp.float32)
    r = c - hi
    mid = r.astype(jnp.bfloat16).astype(jnp.float32)
    lo = (r - mid).astype(jnp.bfloat16).astype(jnp.float32)
    return hi, mid, lo


def _inproj_kernel(x_ref, ng_ref, w_ref, bf_ref, scdw_ref, gq_ref, gk_ref,
                   u_ref, ysc_ref, qa_ref, ka_ref, va_ref, g_ref,
                   wbuf, ccarry, *, ts):
    j = pl.program_id(1)
    ng_ref, w_ref, bf_ref, scdw_ref, gq_ref, gk_ref = (
        r.at[0] for r in (ng_ref, w_ref, bf_ref, scdw_ref, gq_ref, gk_ref))
    nt = (((1,), (1,)), ((), ()))

    def proj(lo, hi):
        return lax.dot_general(hb, w_ref[lo:hi, :], nt, preferred_element_type=jnp.float32)

    def proj_t(lo, hi):
        return lax.dot_general(w_ref[lo:hi, :], hb, nt, preferred_element_type=jnp.float32)

    @pl.when(j == 0)
    def _():
        wbuf[0:SC_HALO, :] = jnp.zeros((SC_HALO, D_SC), jnp.float32)
        ccarry[...] = jnp.zeros_like(ccarry)

    xf = x_ref[0]
    ms = jnp.mean(xf * xf, axis=-1, keepdims=True)
    hb = (xf * lax.rsqrt(ms + EPS) * ng_ref[...]).astype(jnp.bfloat16)

    p1 = proj(0, 3 * D_CF)
    u_ref[0, :, 0:D_CF] = p1[:, 0:D_CF] * _sigmoid(p1[:, D_CF:2 * D_CF])
    u_ref[0, :, D_CF:2 * D_CF] = _silu(p1[:, 2 * D_CF:3 * D_CF])

    p2 = proj(3 * D_CF, N_NAT)
    sc_b = p2[:, 0:D_SC]
    sc_c = p2[:, D_SC:2 * D_SC]
    sc_x = p2[:, 2 * D_SC:3 * D_SC]
    sc_z = p2[:, 3 * D_SC:4 * D_SC]
    wbuf[SC_HALO:SC_HALO + ts, :] = sc_c * sc_x
    sconv = jnp.zeros((ts, D_SC), jnp.float32)
    for k in range(SC_WIDTH):
        off = SC_HALO - (SC_WIDTH - 1) + k
        sconv = sconv + scdw_ref[k:k + 1, :] * wbuf[off:off + ts, :]
    wbuf[0:SC_HALO, :] = wbuf[ts:ts + SC_HALO, :]
    ysc_ref[0] = (sc_b * sconv * _silu(sc_z)).astype(ysc_ref.dtype)

    zf = proj_t(N_NAT + 3 * D_ATT, N_ALL)
    zt = zf[0:D_ATT]
    z = zf[D_ATT:D_ATT + N_HEADS] + bf_ref[...]
    c = jnp.minimum(z, 0.0) - jnp.log1p(jnp.exp(-jnp.abs(z)))
    lane = lax.broadcasted_iota(jnp.int32, (N_HEADS, ts), 1)
    d = 1
    while d < ts:
        c = c + jnp.where(lane >= d, pltpu.roll(c, d, axis=1), 0.0)
        d *= 2
    c = c + ccarry[:, 0:1]
    ccarry[...] = jnp.broadcast_to(c[:, ts - 1:ts], ccarry.shape)
    c = c * LOG2E

    row16 = lax.broadcasted_iota(jnp.int32, (16, ts), 0)
    zpad = jnp.zeros((K_AUG - HEAD_DIM - 16, ts), jnp.float32)

    def head_norm(t, g_ref):
        t3 = t.reshape(N_HEADS, HEAD_DIM, ts)
        ssq = jnp.mean(t3 * t3, axis=1, keepdims=True)
        return (t3 * lax.rsqrt(ssq + EPS)).reshape(D_ATT, ts) * g_ref[...]

    qn = head_norm(proj_t(N_NAT, N_NAT + D_ATT), gq_ref)
    kn = head_norm(proj_t(N_NAT + D_ATT, N_NAT + 2 * D_ATT), gk_ref)
    vt = proj_t(N_NAT + 2 * D_ATT, N_NAT + 3 * D_ATT)
    g_ref[0] = _silu(zt).astype(g_ref.dtype).reshape(N_HEADS, HEAD_DIM, ts)

    ones_row = jnp.where(
        lax.broadcasted_iota(jnp.int32, (V_AUG - HEAD_DIM, ts), 0) == 0, 1.0, 0.0)
    row8 = lax.broadcasted_iota(jnp.int32, (N_HEADS, ts), 0)
    for h in range(N_HEADS):
        c_h = jnp.sum(jnp.where(row8 == h, c, 0.0), axis=0, keepdims=True)
        hi, mid, lo = _split3(c_h)
        csplit = jnp.where(row16 == 0, hi, jnp.where(row16 == 1, mid, lo))
        q_ext = jnp.where(row16 < 3, csplit, jnp.where(row16 < 6, 1.0, 0.0))
        row_m3 = jnp.where(row16 == 3, hi, jnp.where(row16 == 4, mid, lo))
        k_ext = jnp.where(row16 < 3, 1.0, jnp.where(row16 < 6, -row_m3, 0.0))
        lo_h, hi_h = h * HEAD_DIM, (h + 1) * HEAD_DIM
        qa_ref[0, h, 0:HEAD_DIM, :] = qn[lo_h:hi_h].astype(qa_ref.dtype)
        qa_ref[0, h, HEAD_DIM:HEAD_DIM + 16, :] = q_ext.astype(qa_ref.dtype)
        qa_ref[0, h, HEAD_DIM + 16:K_AUG, :] = zpad.astype(qa_ref.dtype)
        k_aug_t = jnp.concatenate([kn[lo_h:hi_h], k_ext, zpad], axis=0)
        ka_ref[0, h] = k_aug_t.T.astype(ka_ref.dtype)
        va_ref[0, h, 0:HEAD_DIM, :] = vt[lo_h:hi_h].astype(va_ref.dtype)
        va_ref[0, h, HEAD_DIM:V_AUG, :] = ones_row.astype(va_ref.dtype)


def _attn_kernel(qa_ref, ka_ref, va_ref, g_ref, y_ref, s_scr, m_scr, *, seq, tq):
    nq = seq // tq
    kpos = lax.broadcasted_iota(jnp.int32, (tq, tq), 0)
    qpos = lax.broadcasted_iota(jnp.int32, (tq, tq), 1)
    causal = kpos <= qpos

    def scores_chunk(h, i, buf, c, m):
        qa = qa_ref[0, h, :, i * tq:(i + 1) * tq]
        s = jnp.dot(ka_ref[0, h, c * tq:(c + 1) * tq, :], qa, preferred_element_type=jnp.float32)
        if c == i:
            s = jnp.where(causal, s, NEG)
        s_scr[buf, c * tq:(c + 1) * tq, :] = s
        mc = jnp.max(s, axis=0, keepdims=True)
        m = mc if m is None else jnp.maximum(m, mc)
        if c == i:
            m_scr[buf] = jnp.broadcast_to(m, (8, tq))
        return m

    def finish_chunk(h, buf, c, acc):
        p = jnp.exp2(s_scr[buf, c * tq:(c + 1) * tq, :] - m_scr[buf, 0:1, :]).astype(jnp.bfloat16)
        d = jnp.dot(va_ref[0, h, :, c * tq:(c + 1) * tq], p, preferred_element_type=jnp.float32)
        return d if acc is None else acc + d

    def pair(hs_cur, i_cur, slot_cur, hs_nxt, i_nxt, slot_nxt):
        ms = [None] * HEAD_GROUP
        accs = [None] * HEAD_GROUP
        for c in range(max(i_cur, i_nxt) + 1):
            for g in range(HEAD_GROUP):
                if c <= i_nxt:
                    ms[g] = scores_chunk(hs_nxt[g], i_nxt, slot_nxt * HEAD_GROUP + g, c, ms[g])
            for g in range(HEAD_GROUP):
                if c <= i_cur:
                    accs[g] = finish_chunk(hs_cur[g], slot_cur * HEAD_GROUP + g, c, accs[g])
        q0 = i_cur * tq
        for g in range(HEAD_GROUP):
            o = accs[g][0:HEAD_DIM] / accs[g][HEAD_DIM:HEAD_DIM + 1]
            gate = g_ref[0, hs_cur[g], :, q0:q0 + tq].astype(jnp.float32)
            y_ref[0, hs_cur[g], :, q0:q0 + tq] = (o * gate).astype(y_ref.dtype)

    order = list(range(nq - 1, -1, -1))
    for g in range(HEAD_GROUP):
        m0 = None
        for c in range(order[0] + 1):
            m0 = scores_chunk(g, order[0], g, c, m0)

    n_groups = N_HEADS // HEAD_GROUP

    def group(hg, carry, last=False):
        hs = [hg * HEAD_GROUP + g for g in range(HEAD_GROUP)]
        hs_next = [(hg + 1) * HEAD_GROUP + g for g in range(HEAD_GROUP)]
        for n, i in enumerate(order):
            if n + 1 < nq:
                pair(hs, i, n % 2, hs, order[n + 1], (n + 1) % 2)
            elif not last:
                pair(hs, i, n % 2, hs_next, order[0], (n + 1) % 2)
            else:
                pair(hs, i, n % 2, None, -1, None)
        return carry

    lax.fori_loop(0, n_groups - 1, group, 0)
    group(n_groups - 1, 0, last=True)


def _outproj_kernel(x_ref, u_ref, ysc_ref, yt_ref, cfdw_ref, cfb_ref, lng_ref, lnb_ref, cfpw_ref,
                    woa_ref, wob_ref, o_ref, ubuf, abuf, pacc, *, ts):
    j = pl.program_id(1)
    cfdw_ref, cfb_ref, lng_ref, lnb_ref, cfpw_ref, woa_ref, wob_ref = (
        r.at[0] for r in (cfdw_ref, cfb_ref, lng_ref, lnb_ref, cfpw_ref, woa_ref, wob_ref))

    @pl.when(j == 0)
    def _():
        ubuf[0:CF_HALO, :] = jnp.zeros((CF_HALO, D_CF), jnp.float32)

    tn = (((0,), (0,)), ((), ()))
    part = lax.dot_general(yt_ref[0], wob_ref[...], tn, preferred_element_type=jnp.float32)
    part = part + jnp.dot(ysc_ref[0], woa_ref[D_CF:D_CF + D_SC, :], preferred_element_type=jnp.float32)
    pacc[...] = part

    ubuf[CF_HALO:CF_HALO + ts, :] = u_ref[0, :, 0:D_CF]
    conv = jnp.broadcast_to(cfb_ref[...], (ts, D_CF))
    base = CF_HALO - (CF_WIDTH - 1)
    for r in range(8):
        taps = [k for k in range(CF_WIDTH) if (base + k) % 8 == r]
        if not taps:
            continue
        a_lo = (base + taps[0]) // 8
        a_hi = (base + taps[-1]) // 8
        rows = 8 * (a_hi - a_lo) + ts
        abuf[0:rows, :] = ubuf[8 * a_lo + r:8 * a_lo + r + rows, :]
        for k in taps:
            a = (base + k) // 8 - a_lo
            conv = conv + cfdw_ref[k:k + 1, :] * abuf[8 * a:8 * a + ts, :]
    ubuf[0:CF_HALO, :] = ubuf[ts:ts + CF_HALO, :]
    mu = jnp.mean(conv, axis=-1, keepdims=True)
    xc = conv - mu
    var = jnp.mean(xc * xc, axis=-1, keepdims=True)
    ln = xc * lax.rsqrt(var + EPS) * lng_ref[...] + lnb_ref[...]
    y_cf = jnp.dot(_silu(ln).astype(jnp.bfloat16), cfpw_ref[...],
                   preferred_element_type=jnp.float32) * u_ref[0, :, D_CF:2 * D_CF]
    last = jnp.dot(y_cf.astype(jnp.bfloat16), woa_ref[0:D_CF, :], preferred_element_type=jnp.float32)
    o_ref[0] = x_ref[0] + pacc[...] + last


def _layer_spec(arr, layer):
    zeros = (0,) * (arr.ndim - 1)
    return pl.BlockSpec((1,) + arr.shape[1:], lambda *_: (layer,) + zeros)


def _inproj_call(x, layer, params, *, ts):
    bsz, seq, _ = x.shape
    bf16 = jnp.bfloat16
    out_shape = (
        jax.ShapeDtypeStruct((bsz, seq, 2 * D_CF), jnp.float32),
        jax.ShapeDtypeStruct((bsz, seq, D_SC), bf16),
        jax.ShapeDtypeStruct((bsz, N_HEADS, K_AUG, seq), bf16),
        jax.ShapeDtypeStruct((bsz, N_HEADS, seq, K_AUG), bf16),
        jax.ShapeDtypeStruct((bsz, N_HEADS, V_AUG, seq), bf16),
        jax.ShapeDtypeStruct((bsz, N_HEADS, HEAD_DIM, seq), bf16),
    )
    in_specs = [pl.BlockSpec((1, ts, D_MODEL), lambda b, j: (b, j, 0))]
    in_specs += [_layer_spec(p, layer) for p in params]
    out_specs = (
        pl.BlockSpec((1, ts, 2 * D_CF), lambda b, j: (b, j, 0)),
        pl.BlockSpec((1, ts, D_SC), lambda b, j: (b, j, 0)),
        pl.BlockSpec((1, N_HEADS, K_AUG, ts), lambda b, j: (b, 0, 0, j)),
        pl.BlockSpec((1, N_HEADS, ts, K_AUG), lambda b, j: (b, 0, j, 0)),
        pl.BlockSpec((1, N_HEADS, V_AUG, ts), lambda b, j: (b, 0, 0, j)),
        pl.BlockSpec((1, N_HEADS, HEAD_DIM, ts), lambda b, j: (b, 0, 0, j)),
    )
    return pl.pallas_call(
        functools.partial(_inproj_kernel, ts=ts),
        grid=(bsz, seq // ts),
        in_specs=in_specs,
        out_specs=out_specs,
        out_shape=out_shape,
        scratch_shapes=[
            pltpu.VMEM((ts + SC_HALO, D_SC), jnp.float32),
            pltpu.VMEM((N_HEADS, 128), jnp.float32),
        ],
        compiler_params=pltpu.CompilerParams(
            dimension_semantics=("arbitrary", "arbitrary"), vmem_limit_bytes=VMEM_LIMIT),
        name="inproj",
    )(x, *params)


def _attn_call(qa, ka, va, g, *, tq):
    bsz, _, _, seq = qa.shape

    def whole(arr):
        return pl.BlockSpec((1,) + arr.shape[1:], lambda b: (b, 0, 0, 0))

    return pl.pallas_call(
        functools.partial(_attn_kernel, seq=seq, tq=tq),
        grid=(bsz,),
        in_specs=[whole(qa), whole(ka), whole(va), whole(g)],
        out_specs=pl.BlockSpec((1, N_HEADS, HEAD_DIM, seq), lambda b: (b, 0, 0, 0)),
        out_shape=jax.ShapeDtypeStruct((bsz, N_HEADS, HEAD_DIM, seq), jnp.bfloat16),
        scratch_shapes=[
            pltpu.VMEM((2 * HEAD_GROUP, seq, tq), jnp.float32),
            pltpu.VMEM((2 * HEAD_GROUP, 8, tq), jnp.float32),
        ],
        compiler_params=pltpu.CompilerParams(
            dimension_semantics=("arbitrary",), vmem_limit_bytes=VMEM_LIMIT),
        name="fox_attention",
    )(qa, ka, va, g)


def _outproj_call(x, u, ysc, yt, layer, cf_params, wo, *, ts):
    bsz, seq, _ = x.shape
    d_nat = D_CF + D_SC
    in_specs = [
        pl.BlockSpec((1, ts, D_MODEL), lambda b, j: (b, j, 0)),
        pl.BlockSpec((1, ts, 2 * D_CF), lambda b, j: (b, j, 0)),
        pl.BlockSpec((1, ts, D_SC), lambda b, j: (b, j, 0)),
        pl.BlockSpec((1, D_ATT, ts), lambda b, j: (b, 0, j)),
    ]
    in_specs += [_layer_spec(p, layer) for p in cf_params]
    in_specs += [
        pl.BlockSpec((1, d_nat, D_MODEL), lambda b, j: (layer, 0, 0)),
        pl.BlockSpec((1, D_ATT, D_MODEL), lambda b, j: (layer, 1, 0)),
    ]
    return pl.pallas_call(
        functools.partial(_outproj_kernel, ts=ts),
        grid=(bsz, seq // ts),
        in_specs=in_specs,
        out_specs=pl.BlockSpec((1, ts, D_MODEL), lambda b, j: (b, j, 0)),
        out_shape=jax.ShapeDtypeStruct(x.shape, x.dtype),
        scratch_shapes=[
            pltpu.VMEM((ts + CF_HALO, D_CF), jnp.float32),
            pltpu.VMEM((ts + CF_HALO, D_CF), jnp.float32),
            pltpu.VMEM((ts, D_MODEL), jnp.float32),
        ],
        compiler_params=pltpu.CompilerParams(
            dimension_semantics=("arbitrary", "arbitrary"), vmem_limit_bytes=VMEM_LIMIT),
        name="outproj",
    )(x, u, ysc, yt, *cf_params, wo, wo)


def kernel(x, norm_g, w_in, b_f, cf_dw, cf_dw_b, cf_ln_g, cf_ln_b, cf_pw, sc_dw, q_norm_g, k_norm_g, w_out):
    depth = norm_g.shape[0]
    seq = x.shape[1]
    ts = min(TS, seq)
    tq = min(TQ, seq)
    assert seq % ts == 0 and seq % tq == 0 and (seq // tq) % 2 == 0
    assert D_CF + D_SC == D_ATT
    bf16 = jnp.bfloat16
    f32 = jnp.float32
    assert w_in.shape[2] == N_ALL
    w_all = jnp.swapaxes(w_in, 1, 2).astype(bf16)
    scale = LOG2E / (HEAD_DIM ** 0.5)
    params = (
        norm_g.reshape(depth, 1, D_MODEL), w_all, b_f.reshape(depth, N_HEADS, 1), sc_dw,
        (q_norm_g.astype(f32) * scale).reshape(depth, D_ATT, 1),
        k_norm_g.astype(f32).reshape(depth, D_ATT, 1),
    )
    cf_params = (
        cf_dw, cf_dw_b.reshape(depth, 1, D_CF), cf_ln_g.reshape(depth, 1, D_CF),
        cf_ln_b.reshape(depth, 1, D_CF), cf_pw.astype(bf16),
    )
    wo = w_out.astype(bf16)
    for layer in range(depth):
        u, ysc, qa, ka, va, g = _inproj_call(x, layer, params, ts=ts)
        yt = _attn_call(qa, ka, va, g, tq=tq).reshape(x.shape[0], D_ATT, seq)
        x = _outproj_call(x, u, ysc, yt, layer, cf_params, wo, ts=min(TS_OUT, seq))
    return x
```

```python
import functools

import jax
import jax.numpy as jnp
from jax import lax
from jax.experimental import pallas as pl
from jax.experimental.pallas import tpu as pltpu

D_MODEL = 1024
D_CF = 256
D_SC = 256
D_ATT = 512
HEAD_DIM = 64
N_HEADS = D_ATT // HEAD_DIM
CF_WIDTH = 31
SC_WIDTH = 3
EPS = 1e-6
N_NAT = 3 * D_CF + 4 * D_SC
N_T = 4 * D_ATT + N_HEADS
N_ALL = N_NAT + N_T

K_AUG = 128
V_AUG = 80
CF_HALO = 32
SC_HALO = 8
NEG = -1e30
LOG2E = 1.4426950408889634

TS = 1024
TS_OUT = 1024
TQ = 256
HEAD_GROUP = 2
SCORE_PAD_ROWS = 8
VMEM_LIMIT = 56 * 1024 * 1024


def _sigmoid(x):
    return 0.5 * jnp.tanh(0.5 * x) + 0.5


def _silu(x):
    hx = 0.5 * x
    return hx * jnp.tanh(hx) + hx


def _split3(c):
    hi = c.astype(jnp.bfloat16).astype(jnp.float32)
    r = c - hi
    mid = r.astype(jnp.bfloat16).astype(jnp.float32)
    lo = (r - mid).astype(jnp.bfloat16).astype(jnp.float32)
    return hi, mid, lo


def _inproj_kernel(x_ref, ng_ref, w_ref, bf_ref, scdw_ref, gq_ref, gk_ref,
                   u_ref, ysc_ref, qa_ref, ka_ref, va_ref, g_ref,
                   wbuf, ccarry, *, ts):
    j = pl.program_id(1)
    ng_ref, w_ref, bf_ref, scdw_ref, gq_ref, gk_ref = (
        r.at[0] for r in (ng_ref, w_ref, bf_ref, scdw_ref, gq_ref, gk_ref))
    nt = (((1,), (1,)), ((), ()))

    def proj(lo, hi):
        return lax.dot_general(hb, w_ref[lo:hi, :], nt, preferred_element_type=jnp.float32)

    def proj_t(lo, hi):
        return lax.dot_general(w_ref[lo:hi, :], hb, nt, preferred_element_type=jnp.float32)

    @pl.when(j == 0)
    def _():
        wbuf[0:SC_HALO, :] = jnp.zeros((SC_HALO, D_SC), jnp.float32)
        ccarry[...] = jnp.zeros_like(ccarry)

    xf = x_ref[0]
    ms = jnp.mean(xf * xf, axis=-1, keepdims=True)
    hb = (xf * lax.rsqrt(ms + EPS) * ng_ref[...]).astype(jnp.bfloat16)

    p1 = proj(0, 3 * D_CF)
    u_ref[0, :, 0:D_CF] = p1[:, 0:D_CF] * _sigmoid(p1[:, D_CF:2 * D_CF])
    u_ref[0, :, D_CF:2 * D_CF] = _silu(p1[:, 2 * D_CF:3 * D_CF])

    p2 = proj(3 * D_CF, N_NAT)
    sc_b = p2[:, 0:D_SC]
    sc_c = p2[:, D_SC:2 * D_SC]
    sc_x = p2[:, 2 * D_SC:3 * D_SC]
    sc_z = p2[:, 3 * D_SC:4 * D_SC]
    wbuf[SC_HALO:SC_HALO + ts, :] = sc_c * sc_x
    sconv = jnp.zeros((ts, D_SC), jnp.float32)
    for k in range(SC_WIDTH):
        off = SC_HALO - (SC_WIDTH - 1) + k
        sconv = sconv + scdw_ref[k:k + 1, :] * wbuf[off:off + ts, :]
    wbuf[0:SC_HALO, :] = wbuf[ts:ts + SC_HALO, :]
    ysc_ref[0] = (sc_b * sconv * _silu(sc_z)).astype(ysc_ref.dtype)

    zf = proj_t(N_NAT + 3 * D_ATT, N_ALL)
    zt = zf[0:D_ATT]
    z = zf[D_ATT:D_ATT + N_HEADS] + bf_ref[...]
    c = jnp.minimum(z, 0.0) - jnp.log1p(jnp.exp(-jnp.abs(z)))
    lane = lax.broadcasted_iota(jnp.int32, (N_HEADS, ts), 1)
    d = 1
    while d < ts:
        c = c + jnp.where(lane >= d, pltpu.roll(c, d, axis=1), 0.0)
        d *= 2
    c = c + ccarry[:, 0:1]
    ccarry[...] = jnp.broadcast_to(c[:, ts - 1:ts], ccarry.shape)
    c = c * LOG2E

    row16 = lax.broadcasted_iota(jnp.int32, (16, ts), 0)
    zpad = jnp.zeros((K_AUG - HEAD_DIM - 16, ts), jnp.float32)

    def head_norm(t, g_ref):
        t3 = t.reshape(N_HEADS, HEAD_DIM, ts)
        ssq = jnp.mean(t3 * t3, axis=1, keepdims=True)
        return (t3 * lax.rsqrt(ssq + EPS)).reshape(D_ATT, ts) * g_ref[...]

    qn = head_norm(proj_t(N_NAT, N_NAT + D_ATT), gq_ref)
    kn = head_norm(proj_t(N_NAT + D_ATT, N_NAT + 2 * D_ATT), gk_ref)
    vt = proj_t(N_NAT + 2 * D_ATT, N_NAT + 3 * D_ATT)
    g_ref[0] = _silu(zt).astype(g_ref.dtype).reshape(N_HEADS, HEAD_DIM, ts)

    ones_row = jnp.where(
        lax.broadcasted_iota(jnp.int32, (V_AUG - HEAD_DIM, ts), 0) == 0, 1.0, 0.0)
    row8 = lax.broadcasted_iota(jnp.int32, (N_HEADS, ts), 0)
    for h in range(N_HEADS):
        c_h = jnp.sum(jnp.where(row8 == h, c, 0.0), axis=0, keepdims=True)
        hi, mid, lo = _split3(c_h)
        csplit = jnp.where(row16 == 0, hi, jnp.where(row16 == 1, mid, lo))
        q_ext = jnp.where(row16 < 3, csplit, jnp.where(row16 < 6, 1.0, 0.0))
        row_m3 = jnp.where(row16 == 3, hi, jnp.where(row16 == 4, mid, lo))
        k_ext = jnp.where(row16 < 3, 1.0, jnp.where(row16 < 6, -row_m3, 0.0))
        lo_h, hi_h = h * HEAD_DIM, (h + 1) * HEAD_DIM
        qa_ref[0, h, 0:HEAD_DIM, :] = qn[lo_h:hi_h].astype(qa_ref.dtype)
        qa_ref[0, h, HEAD_DIM:HEAD_DIM + 16, :] = q_ext.astype(qa_ref.dtype)
        qa_ref[0, h, HEAD_DIM + 16:K_AUG, :] = zpad.astype(qa_ref.dtype)
        k_aug_t = jnp.concatenate([kn[lo_h:hi_h], k_ext, zpad], axis=0)
        ka_ref[0, h] = k_aug_t.T.astype(ka_ref.dtype)
        va_ref[0, h, 0:HEAD_DIM, :] = vt[lo_h:hi_h].astype(va_ref.dtype)
        va_ref[0, h, HEAD_DIM:V_AUG, :] = ones_row.astype(va_ref.dtype)


def _attn_kernel(qa_ref, ka_ref, va_ref, g_ref, y_ref, s_scr, m_scr, *, seq, tq):
    nq = seq // tq
    kpos = lax.broadcasted_iota(jnp.int32, (tq, tq), 0)
    qpos = lax.broadcasted_iota(jnp.int32, (tq, tq), 1)
    causal = kpos <= qpos

    def scores_chunk(h, i, buf, c, m):
        qa = qa_ref[0, h, :, i * tq:(i + 1) * tq]
        s = jnp.dot(ka_ref[0, h, c * tq:(c + 1) * tq, :], qa, preferred_element_type=jnp.float32)
        if c == i:
            s = jnp.where(causal, s, NEG)
        s_scr[buf, c * tq:(c + 1) * tq, :] = s
        mc = jnp.max(s, axis=0, keepdims=True)
        m = mc if m is None else jnp.maximum(m, mc)
        if c == i:
            m_scr[buf] = jnp.broadcast_to(m, (8, tq))
        return m

    def finish_chunk(h, buf, c, acc):
        p = jnp.exp2(s_scr[buf, c * tq:(c + 1) * tq, :] - m_scr[buf, 0:1, :]).astype(jnp.bfloat16)
        d = jnp.dot(va_ref[0, h, :, c * tq:(c + 1) * tq], p, preferred_element_type=jnp.float32)
        return d if acc is None else acc + d

    def pair(hs_cur, i_cur, slot_cur, hs_nxt, i_nxt, slot_nxt):
        ms = [None] * HEAD_GROUP
        accs = [None] * HEAD_GROUP
        for c in range(max(i_cur, i_nxt) + 1):
            for g in range(HEAD_GROUP):
                if c <= i_nxt:
                    ms[g] = scores_chunk(hs_nxt[g], i_nxt, slot_nxt * HEAD_GROUP + g, c, ms[g])
            for g in range(HEAD_GROUP):
                if c <= i_cur:
                    accs[g] = finish_chunk(hs_cur[g], slot_cur * HEAD_GROUP + g, c, accs[g])
        q0 = i_cur * tq
        for g in range(HEAD_GROUP):
            o = accs[g][0:HEAD_DIM] / accs[g][HEAD_DIM:HEAD_DIM + 1]
            gate = g_ref[0, hs_cur[g], :, q0:q0 + tq].astype(jnp.float32)
            y_ref[0, hs_cur[g], :, q0:q0 + tq] = (o * gate).astype(y_ref.dtype)

    order = list(range(nq - 1, -1, -1))
    for g in range(HEAD_GROUP):
        m0 = None
        for c in range(order[0] + 1):
            m0 = scores_chunk(g, order[0], g, c, m0)

    n_groups = N_HEADS // HEAD_GROUP

    def group(hg, carry, last=False):
        hs = [hg * HEAD_GROUP + g for g in range(HEAD_GROUP)]
        hs_next = [(hg + 1) * HEAD_GROUP + g for g in range(HEAD_GROUP)]
        for n, i in enumerate(order):
            if n + 1 < nq:
                pair(hs, i, n % 2, hs, order[n + 1], (n + 1) % 2)
            elif not last:
                pair(hs, i, n % 2, hs_next, order[0], (n + 1) % 2)
            else:
                pair(hs, i, n % 2, None, -1, None)
        return carry

    lax.fori_loop(0, n_groups - 1, group, 0)
    group(n_groups - 1, 0, last=True)


def _outproj_kernel(x_ref, u_ref, ysc_ref, yt_ref, cfdw_ref, cfb_ref, lng_ref, lnb_ref, cfpw_ref,
                    woa_ref, wob_ref, o_ref, ubuf, abuf, pacc, *, ts):
    j = pl.program_id(1)
    cfdw_ref, cfb_ref, lng_ref, lnb_ref, cfpw_ref, woa_ref, wob_ref = (
        r.at[0] for r in (cfdw_ref, cfb_ref, lng_ref, lnb_ref, cfpw_ref, woa_ref, wob_ref))

    @pl.when(j == 0)
    def _():
        ubuf[0:CF_HALO, :] = jnp.zeros((CF_HALO, D_CF), jnp.float32)

    tn = (((0,), (0,)), ((), ()))
    part = lax.dot_general(yt_ref[0], wob_ref[...], tn, preferred_element_type=jnp.float32)
    part = part + jnp.dot(ysc_ref[0], woa_ref[D_CF:D_CF + D_SC, :], preferred_element_type=jnp.float32)
    pacc[...] = part

    ubuf[CF_HALO:CF_HALO + ts, :] = u_ref[0, :, 0:D_CF]
    conv = jnp.broadcast_to(cfb_ref[...], (ts, D_CF))
    base = CF_HALO - (CF_WIDTH - 1)
    for r in range(8):
        taps = [k for k in range(CF_WIDTH) if (base + k) % 8 == r]
        if not taps:
            continue
        a_lo = (base + taps[0]) // 8
        a_hi = (base + taps[-1]) // 8
        rows = 8 * (a_hi - a_lo) + ts
        abuf[0:rows, :] = ubuf[8 * a_lo + r:8 * a_lo + r + rows, :]
        for k in taps:
            a = (base + k) // 8 - a_lo
            conv = conv + cfdw_ref[k:k + 1, :] * abuf[8 * a:8 * a + ts, :]
    ubuf[0:CF_HALO, :] = ubuf[ts:ts + CF_HALO, :]
    mu = jnp.mean(conv, axis=-1, keepdims=True)
    xc = conv - mu
    var = jnp.mean(xc * xc, axis=-1, keepdims=True)
    ln = xc * lax.rsqrt(var + EPS) * lng_ref[...] + lnb_ref[...]
    y_cf = jnp.dot(_silu(ln).astype(jnp.bfloat16), cfpw_ref[...],
                   preferred_element_type=jnp.float32) * u_ref[0, :, D_CF:2 * D_CF]
    last = jnp.dot(y_cf.astype(jnp.bfloat16), woa_ref[0:D_CF, :], preferred_element_type=jnp.float32)
    o_ref[0] = x_ref[0] + pacc[...] + last


def _layer_spec(arr, layer):
    zeros = (0,) * (arr.ndim - 1)
    return pl.BlockSpec((1,) + arr.shape[1:], lambda *_: (layer,) + zeros)


def _inproj_call(x, layer, params, *, ts):
    bsz, seq, _ = x.shape
    bf16 = jnp.bfloat16
    out_shape = (
        jax.ShapeDtypeStruct((bsz, seq, 2 * D_CF), jnp.float32),
        jax.ShapeDtypeStruct((bsz, seq, D_SC), bf16),
        jax.ShapeDtypeStruct((bsz, N_HEADS, K_AUG, seq), bf16),
        jax.ShapeDtypeStruct((bsz, N_HEADS, seq, K_AUG), bf16),
        jax.ShapeDtypeStruct((bsz, N_HEADS, V_AUG, seq), bf16),
        jax.ShapeDtypeStruct((bsz, N_HEADS, HEAD_DIM, seq), bf16),
    )
    in_specs = [pl.BlockSpec((1, ts, D_MODEL), lambda b, j: (b, j, 0))]
    in_specs += [_layer_spec(p, layer) for p in params]
    out_specs = (
        pl.BlockSpec((1, ts, 2 * D_CF), lambda b, j: (b, j, 0)),
        pl.BlockSpec((1, ts, D_SC), lambda b, j: (b, j, 0)),
        pl.BlockSpec((1, N_HEADS, K_AUG, ts), lambda b, j: (b, 0, 0, j)),
        pl.BlockSpec((1, N_HEADS, ts, K_AUG), lambda b, j: (b, 0, j, 0)),
        pl.BlockSpec((1, N_HEADS, V_AUG, ts), lambda b, j: (b, 0, 0, j)),
        pl.BlockSpec((1, N_HEADS, HEAD_DIM, ts), lambda b, j: (b, 0, 0, j)),
    )
    return pl.pallas_call(
        functools.partial(_inproj_kernel, ts=ts),
        grid=(bsz, seq // ts),
        in_specs=in_specs,
        out_specs=out_specs,
        out_shape=out_shape,
        scratch_shapes=[
            pltpu.VMEM((ts + SC_HALO, D_SC), jnp.float32),
            pltpu.VMEM((N_HEADS, 128), jnp.float32),
        ],
        compiler_params=pltpu.CompilerParams(
            dimension_semantics=("arbitrary", "arbitrary"), vmem_limit_bytes=VMEM_LIMIT),
        name="inproj",
    )(x, *params)


def _attn_call(qa, ka, va, g, *, tq):
    bsz, _, _, seq = qa.shape

    def whole(arr):
        return pl.BlockSpec((1,) + arr.shape[1:], lambda b: (b, 0, 0, 0))

    return pl.pallas_call(
        functools.partial(_attn_kernel, seq=seq, tq=tq),
        grid=(bsz,),
        in_specs=[whole(qa), whole(ka), whole(va), whole(g)],
        out_specs=pl.BlockSpec((1, N_HEADS, HEAD_DIM, seq), lambda b: (b, 0, 0, 0)),
        out_shape=jax.ShapeDtypeStruct((bsz, N_HEADS, HEAD_DIM, seq), jnp.bfloat16),
        scratch_shapes=[
            pltpu.VMEM((2 * HEAD_GROUP, seq + SCORE_PAD_ROWS, tq), jnp.float32),
            pltpu.VMEM((2 * HEAD_GROUP, 8, tq), jnp.float32),
        ],
        compiler_params=pltpu.CompilerParams(
            dimension_semantics=("arbitrary",), vmem_limit_bytes=VMEM_LIMIT),
        name="fox_attention",
    )(qa, ka, va, g)


def _outproj_call(x, u, ysc, yt, layer, cf_params, wo, *, ts):
    bsz, seq, _ = x.shape
    d_nat = D_CF + D_SC
    in_specs = [
        pl.BlockSpec((1, ts, D_MODEL), lambda b, j: (b, j, 0)),
        pl.BlockSpec((1, ts, 2 * D_CF), lambda b, j: (b, j, 0)),
        pl.BlockSpec((1, ts, D_SC), lambda b, j: (b, j, 0)),
        pl.BlockSpec((1, D_ATT, ts), lambda b, j: (b, 0, j)),
    ]
    in_specs += [_layer_spec(p, layer) for p in cf_params]
    in_specs += [
        pl.BlockSpec((1, d_nat, D_MODEL), lambda b, j: (layer, 0, 0)),
        pl.BlockSpec((1, D_ATT, D_MODEL), lambda b, j: (layer, 1, 0)),
    ]
    return pl.pallas_call(
        functools.partial(_outproj_kernel, ts=ts),
        grid=(bsz, seq // ts),
        in_specs=in_specs,
        out_specs=pl.BlockSpec((1, ts, D_MODEL), lambda b, j: (b, j, 0)),
        out_shape=jax.ShapeDtypeStruct(x.shape, x.dtype),
        scratch_shapes=[
            pltpu.VMEM((ts + CF_HALO, D_CF), jnp.float32),
            pltpu.VMEM((ts + CF_HALO, D_CF), jnp.float32),
            pltpu.VMEM((ts, D_MODEL), jnp.float32),
        ],
        compiler_params=pltpu.CompilerParams(
            dimension_semantics=("arbitrary", "arbitrary"), vmem_limit_bytes=VMEM_LIMIT),
        name="outproj",
    )(x, u, ysc, yt, *cf_params, wo, wo)


def kernel(x, norm_g, w_in, b_f, cf_dw, cf_dw_b, cf_ln_g, cf_ln_b, cf_pw, sc_dw, q_norm_g, k_norm_g, w_out):
    depth = norm_g.shape[0]
    seq = x.shape[1]
    ts = min(TS, seq)
    tq = min(TQ, seq)
    assert seq % ts == 0 and seq % tq == 0 and (seq // tq) % 2 == 0
    assert D_CF + D_SC == D_ATT
    bf16 = jnp.bfloat16
    f32 = jnp.float32
    assert w_in.shape[2] == N_ALL
    w_all = jnp.swapaxes(w_in, 1, 2).astype(bf16)
    scale = LOG2E / (HEAD_DIM ** 0.5)
    params = (
        norm_g.reshape(depth, 1, D_MODEL), w_all, b_f.reshape(depth, N_HEADS, 1), sc_dw,
        (q_norm_g.astype(f32) * scale).reshape(depth, D_ATT, 1),
        k_norm_g.astype(f32).reshape(depth, D_ATT, 1),
    )
    cf_params = (
        cf_dw, cf_dw_b.reshape(depth, 1, D_CF), cf_ln_g.reshape(depth, 1, D_CF),
        cf_ln_b.reshape(depth, 1, D_CF), cf_pw.astype(bf16),
    )
    wo = w_out.astype(bf16)
    for layer in range(depth):
        u, ysc, qa, ka, va, g = _inproj_call(x, layer, params, ts=ts)
        yt = _attn_call(qa, ka, va, g, tq=tq).reshape(x.shape[0], D_ATT, seq)
        x = _outproj_call(x, u, ysc, yt, layer, cf_params, wo, ts=min(TS_OUT, seq))
    return x
```
